```python
import math
import jax, jax.numpy as jnp
from jax import lax
import numpy as np

D_MODEL = 1024
BATCH = 4
SEQ = 8192
DEPTH = 1

GRID_W = 64
HY_WIDTH = 512
HY_ORDER = 2
SHORT_K = 3
FILTER_EMB = 33
FILTER_HIDDEN = 64
HY_QUICK_DECAY_PCT = 0.3
HY_GRADUAL_DECAY_PCT = 1.5
HY_DECAY_TARGET = 1e-2
N_HEADS = 8
N_KV_HEADS = 2
HEAD_DIM = 64
Q_BLOCK = 128
ROPE_THETA = 10000.0
N_EXPERTS = 16
EC_FACTOR = 2
D_FF_EXPERT = 2048
EPS = 1e-6
N_BRANCHES = 2
ATT_WIDTH = N_HEADS * HEAD_DIM
KV_WIDTH = N_KV_HEADS * HEAD_DIM
HY_IN = 3 * HY_WIDTH
IN_WIDTH = HY_IN + ATT_WIDTH + 2 * KV_WIDTH + N_BRANCHES * D_MODEL

kernel_name = "hybrid_hyena_gqa_ec_moe_block"

F32 = jnp.float32


def rms_norm(x, g):
    xf = x.astype(F32)
    y = xf * lax.rsqrt(jnp.mean(xf * xf, axis=-1, keepdims=True) + EPS)
    return (y * g.astype(F32)).astype(x.dtype)


def short_conv_centred(u, w, b):
    L = u.shape[1]
    pad = SHORT_K // 2
    up = jnp.pad(u, ((0, 0), (pad, pad), (0, 0)))
    y = b
    for j in range(SHORT_K):
        y = y + up[:, j:j + L] * w[j]
    return y


def hyena_filters(L, w1, b1, w2, b2, w3, b3, w4, freq):
    t01 = jnp.linspace(0.0, 1.0, L, dtype=F32)[:, None]
    bands = (FILTER_EMB - 1) // 2
    f = jnp.linspace(1e-4, bands - 1, bands, dtype=F32)[None, :]
    w = (2.0 * math.pi) * jnp.arange(L, dtype=F32)[:, None] / L
    z = jnp.concatenate([t01, jnp.cos(f * w), -jnp.sin(f * w)], axis=-1)
    fr = freq.astype(F32)
    hdn = jnp.sin(fr[0] * (z @ w1.astype(F32) + b1.astype(F32)))
    hdn = jnp.sin(fr[1] * (hdn @ w2.astype(F32) + b2.astype(F32)))
    hdn = jnp.sin(fr[2] * (hdn @ w3.astype(F32) + b3.astype(F32)))
    h = (hdn @ w4.astype(F32)).reshape(L, HY_ORDER, 2, HY_WIDTH)
    max_decay = math.log(HY_DECAY_TARGET) / HY_QUICK_DECAY_PCT
    min_decay = math.log(HY_DECAY_TARGET) / HY_GRADUAL_DECAY_PCT
    deltas = jnp.linspace(min_decay, max_decay, HY_WIDTH, dtype=F32)
    h = h * jnp.exp(-t01[:, :, None, None] * jnp.abs(deltas))
    h = h / (jnp.sum(jnp.abs(h), axis=0, keepdims=True) + EPS)
    return h


def bidir_fftconv(u, h_fwd, h_bwd):
    L = u.shape[1]
    n = 2 * L
    y_f = jnp.fft.irfft(jnp.fft.rfft(u, n=n, axis=1) * jnp.fft.rfft(h_fwd, n=n, axis=0)[None], n=n, axis=1)[:, :L]
    ur = u[:, ::-1]
    y_b = jnp.fft.irfft(jnp.fft.rfft(ur, n=n, axis=1) * jnp.fft.rfft(h_bwd, n=n, axis=0)[None], n=n, axis=1)[:, :L][:, ::-1]
    return y_f + y_b


def hyena_mixer(z, short_w, short_b, w1, b1, w2, b2, w3, b3, w4, freq, filt_bias):
    L = z.shape[1]
    zc = short_conv_centred(z, short_w, short_b).astype(F32)
    v, x1, x2 = jnp.split(zc, 3, axis=-1)
    h = hyena_filters(L, w1, b1, w2, b2, w3, b3, w4, freq)
    fb = filt_bias.astype(F32)
    gates = (x1, x2)
    y = v
    for o in range(HY_ORDER):
        y = gates[o] * (bidir_fftconv(y, h[:, o, 0], h[:, o, 1]) + fb[o] * y)
    return y.astype(z.dtype)


def axial_rope_tables(L):
    rows = L // GRID_W
    row = jnp.repeat(jnp.arange(rows), GRID_W).astype(F32)
    col = jnp.tile(jnp.arange(GRID_W), rows).astype(F32)
    half = HEAD_DIM // 2
    inv = ROPE_THETA ** (-jnp.arange(0, half, 2, dtype=F32) / half)
    ang = jnp.concatenate([row[:, None] * inv, col[:, None] * inv], axis=-1)
    return jnp.cos(ang), jnp.sin(ang)


def apply_axial_rope(x, cos, sin):
    half = HEAD_DIM // 2
    quarter = half // 2

    def rot(u, c, s):
        u1, u2 = u[..., :quarter], u[..., quarter:]
        return jnp.concatenate([u1 * c - u2 * s, u1 * s + u2 * c], axis=-1)

    return jnp.concatenate([rot(x[..., :half], cos[:, :quarter], sin[:, :quarter]),
                            rot(x[..., half:], cos[:, quarter:], sin[:, quarter:])], axis=-1)


def gqa_attention(q, k, v, q_gain, k_gain):
    B, L, _ = q.shape
    G = N_HEADS // N_KV_HEADS
    q = q.reshape(B, L, N_KV_HEADS, G, HEAD_DIM).transpose(0, 2, 3, 1, 4)
    k = k.reshape(B, L, N_KV_HEADS, HEAD_DIM).transpose(0, 2, 1, 3)
    v = v.reshape(B, L, N_KV_HEADS, HEAD_DIM).transpose(0, 2, 1, 3)
    cos, sin = axial_rope_tables(L)
    q = (apply_axial_rope(rms_norm(q, q_gain).astype(F32), cos, sin) * (HEAD_DIM ** -0.5)).astype(v.dtype)
    k = apply_axial_rope(rms_norm(k, k_gain).astype(F32), cos, sin).astype(v.dtype)
    nb = L // Q_BLOCK
    qb = q.reshape(B, N_KV_HEADS, G, nb, Q_BLOCK, HEAD_DIM).transpose(3, 0, 1, 2, 4, 5)

    def block(qblk):
        s = jnp.einsum('bkgqd,bksd->bkgqs', qblk, k).astype(F32)
        p = jax.nn.softmax(s, axis=-1).astype(v.dtype)
        return jnp.einsum('bkgqs,bksd->bkgqd', p, v)

    o = lax.map(block, qb)
    return o.transpose(1, 0, 4, 2, 3, 5).reshape(B, L, ATT_WIDTH)


def expert_choice_ffn(h, w_router, w_gate, w_up, w_down):
    B, L, D = h.shape
    cap = EC_FACTOR * L // N_EXPERTS
    aff = jax.nn.softmax((h @ w_router).astype(F32), axis=-1)
    top_w, top_i = lax.top_k(jnp.swapaxes(aff, 1, 2), cap)
    xe = jax.vmap(lambda hb, ib: hb[ib])(h, top_i)
    a = jnp.einsum('becd,edf->becf', xe, w_gate)
    u = jnp.einsum('becd,edf->becf', xe, w_up)
    ye = jnp.einsum('becf,efd->becd', jax.nn.silu(a) * u, w_down)
    ye = ye * top_w[..., None].astype(ye.dtype)
    return jax.vmap(lambda ib, yb: jnp.zeros((L, D), yb.dtype).at[ib.reshape(-1)].add(yb.reshape(-1, D)))(top_i, ye)


def setup_inputs(seed: int = 0) -> dict:
    key = jax.random.key(seed)
    ks = jax.random.split(key, 32)
    D = D_MODEL

    def nrm(k, shape, scale):
        return jax.random.normal(k, shape, F32) * scale

    return {
        "x": nrm(ks[0], (BATCH, SEQ, D), 1.0),
        "c": nrm(ks[1], (BATCH, D), 1.0),
        "w_ada": nrm(ks[2], (DEPTH, D, 6 * D), 0.5 * D ** -0.5),
        "b_ada": nrm(ks[3], (DEPTH, 6 * D), 0.02),
        "g_mix": 1.0 + nrm(ks[4], (DEPTH, D), 0.02),
        "g_ffn": 1.0 + nrm(ks[5], (DEPTH, D), 0.02),
        "w_in": nrm(ks[6], (DEPTH, D, IN_WIDTH), D ** -0.5),
        "b_in": nrm(ks[7], (DEPTH, IN_WIDTH), 0.02),
        "short_w": nrm(ks[8], (DEPTH, SHORT_K, HY_IN), SHORT_K ** -0.5),
        "short_b": nrm(ks[9], (DEPTH, HY_IN), 0.02),
        "hy_w1": nrm(ks[10], (DEPTH, FILTER_EMB, FILTER_HIDDEN), FILTER_EMB ** -0.5),
        "hy_b1": nrm(ks[11], (DEPTH, FILTER_HIDDEN), 0.02),
        "hy_w2": nrm(ks[12], (DEPTH, FILTER_HIDDEN, FILTER_HIDDEN), FILTER_HIDDEN ** -0.5),
        "hy_b2": nrm(ks[13], (DEPTH, FILTER_HIDDEN), 0.02),
        "hy_w3": nrm(ks[14], (DEPTH, FILTER_HIDDEN, FILTER_HIDDEN), FILTER_HIDDEN ** -0.5),
        "hy_b3": nrm(ks[15], (DEPTH, FILTER_HIDDEN), 0.02),
        "hy_w4": nrm(ks[16], (DEPTH, FILTER_HIDDEN, HY_ORDER * 2 * HY_WIDTH), FILTER_HIDDEN ** -0.5),
        "hy_freq": 1.0 + nrm(ks[17], (DEPTH, 3, FILTER_HIDDEN), 0.02),
        "hy_bias": nrm(ks[18], (DEPTH, HY_ORDER, HY_WIDTH), 0.1),
        "q_gain": 1.0 + nrm(ks[19], (DEPTH, HEAD_DIM), 0.02),
        "k_gain": 1.0 + nrm(ks[20], (DEPTH, HEAD_DIM), 0.02),
        "w_hy_out": nrm(ks[21], (DEPTH, HY_WIDTH, D), HY_WIDTH ** -0.5),
        "w_att_out": nrm(ks[22], (DEPTH, ATT_WIDTH, D), ATT_WIDTH ** -0.5),
        "w_out": nrm(ks[23], (DEPTH, D, D), D ** -0.5),
        "w_router": nrm(ks[24], (DEPTH, D, N_EXPERTS), D ** -0.5),
        "w_gate": nrm(ks[25], (DEPTH, N_EXPERTS, D, D_FF_EXPERT), D ** -0.5),
        "w_up": nrm(ks[26], (DEPTH, N_EXPERTS, D, D_FF_EXPERT), D ** -0.5),
        "w_down": nrm(ks[27], (DEPTH, N_EXPERTS, D_FF_EXPERT, D), D_FF_EXPERT ** -0.5),
    }


def reference(x, c, w_ada, b_ada, g_mix, g_ffn, w_in, b_in, short_w, short_b,
              hy_w1, hy_b1, hy_w2, hy_b2, hy_w3, hy_b3, hy_w4, hy_freq, hy_bias,
              q_gain, k_gain, w_hy_out, w_att_out, w_out,
              w_router, w_gate, w_up, w_down):
    split_at = [HY_IN, HY_IN + ATT_WIDTH, HY_IN + ATT_WIDTH + KV_WIDTH, HY_IN + ATT_WIDTH + 2 * KV_WIDTH]
    for l in range(DEPTH):
        mod = (jax.nn.silu(c) @ w_ada[l] + b_ada[l])[:, None, :]
        sh1, sc1, gt1, sh2, sc2, gt2 = jnp.split(mod, 6, axis=-1)

        h = rms_norm(x, g_mix[l]) * (1.0 + sc1) + sh1
        z = h @ w_in[l] + b_in[l]
        z_hy, q, k, v, z_g = jnp.split(z, split_at, axis=-1)
        y_hy = hyena_mixer(z_hy, short_w[l], short_b[l], hy_w1[l], hy_b1[l], hy_w2[l], hy_b2[l],
                           hy_w3[l], hy_b3[l], hy_w4[l], hy_freq[l], hy_bias[l]) @ w_hy_out[l]
        y_at = gqa_attention(q, k, v, q_gain[l], k_gain[l]) @ w_att_out[l]
        g_hy, g_at = jnp.split(jax.nn.sigmoid(z_g), N_BRANCHES, axis=-1)
        x = x + gt1 * ((g_hy * y_hy + g_at * y_at) @ w_out[l])

        h2 = rms_norm(x, g_ffn[l]) * (1.0 + sc2) + sh2
        x = x + gt2 * expert_choice_ffn(h2, w_router[l], w_gate[l], w_up[l], w_down[l])
    return x
```

```python
import functools
import math

import numpy as np
import jax
import jax.numpy as jnp
from jax import lax
from jax.experimental import pallas as pl
from jax.experimental.pallas import tpu as pltpu

F32 = jnp.float32
BF16 = jnp.bfloat16
I32 = jnp.int32
HIGHEST = lax.Precision.HIGHEST

EPS = 1e-6
GRID_W = 64
HY_WIDTH = 512
HY_ORDER = 2
SHORT_K = 3
FILTER_EMB = 33
FILTER_HIDDEN = 64
HY_QUICK_DECAY_PCT = 0.3
HY_GRADUAL_DECAY_PCT = 1.5
HY_DECAY_TARGET = 1e-2
N_HEADS = 8
N_KV_HEADS = 2
HEAD_DIM = 64
ROPE_THETA = 10000.0
N_EXPERTS = 16
EC_FACTOR = 2
HY_IN = 3 * HY_WIDTH
ATT_WIDTH = N_HEADS * HEAD_DIM
KV_WIDTH = N_KV_HEADS * HEAD_DIM

LANES = 128
VMEM_LIMIT = 56 * 1024 * 1024
TOKEN_TILE = 256
Q_TILE = 128
KV_CHUNK = 1024
HY_CH_BLOCK = 8
FILTER_ROWS = 128
A_ROWS = 16
F32_MIN_EXP = -150.0
TOPK_EXP_STEPS = 10
TOPK_VALUE_STEPS = 28
COMBINE_SPLIT = 4
ONES_ROWS = 16
Q_SCALE = HEAD_DIM ** -0.5 * math.log2(math.e)

NT_DIMS = (((1,), (1,)), ((), ()))
TN_DIMS = (((0,), (0,)), ((), ()))


def _params(*sem):
    return pltpu.CompilerParams(dimension_semantics=sem, vmem_limit_bytes=VMEM_LIMIT)


def _adaln_kernel(c_ref, w_ref, b_ref, o_ref):
    c = c_ref[...]
    s = c * jax.nn.sigmoid(c)
    o_ref[...] = jnp.dot(s, w_ref[...], precision=HIGHEST, preferred_element_type=F32) + b_ref[...]


def _adaln(c, w, b):
    bsz, d = c.shape
    n = w.shape[1]
    rows = 8
    cp = jnp.zeros((rows, d), F32).at[:bsz].set(c)
    tn = 1536
    out = pl.pallas_call(
        _adaln_kernel,
        grid=(n // tn,),
        in_specs=[pl.BlockSpec((rows, d), lambda j: (0, 0)),
                  pl.BlockSpec((d, tn), lambda j: (0, j)),
                  pl.BlockSpec((1, tn), lambda j: (0, j))],
        out_specs=pl.BlockSpec((rows, tn), lambda j: (0, j)),
        out_shape=jax.ShapeDtypeStruct((rows, n), F32),
        compiler_params=_params("parallel"),
        name="adaln",
    )(cp, w, b.reshape(1, n))
    return out[:bsz]


def _inproj_kernel(x_ref, mod_ref, g_ref, whyT_ref, bhy_ref, wr_ref, br_ref, cos_ref, sin_ref,
                   qg_ref, kg_ref, bd_ref, zhyT_ref, q_ref, k_ref, v_ref, gate_ref):
    x = x_ref[0]
    ms = jnp.mean(x * x, axis=-1, keepdims=True)
    sh1 = mod_ref[0, 0:1, :]
    sc1 = mod_ref[0, 1:2, :]
    h = (x * lax.rsqrt(ms + EPS)) * g_ref[...] * (1.0 + sc1) + sh1
    hb = h.astype(BF16)
    zhyT_ref[0] = lax.dot_general(whyT_ref[...], hb, NT_DIMS, preferred_element_type=F32) + bhy_ref[...]
    zr = jnp.dot(hb, wr_ref[...], preferred_element_type=F32) + br_ref[...]

    cos = cos_ref[...]
    sin = sin_ref[...]
    bd = bd_ref[...]
    lane = lax.broadcasted_iota(I32, cos.shape, 1)
    first = (lane & (HEAD_DIM // 2 - 1)) < HEAD_DIM // 4

    def norm_rope(u, gain, scale):
        ss = jnp.dot(u * u, bd, precision=HIGHEST, preferred_element_type=F32)
        un = (u * lax.rsqrt(ss * (1.0 / HEAD_DIM) + EPS)) * gain
        rot = jnp.where(first, pltpu.roll(un, LANES - HEAD_DIM // 4, 1), pltpu.roll(un, HEAD_DIM // 4, 1))
        return (un * cos + rot * sin) * scale

    for j in range(ATT_WIDTH // LANES):
        u = zr[:, j * LANES:(j + 1) * LANES]
        q_ref[0, :, j * LANES:(j + 1) * LANES] = norm_rope(u, qg_ref[...], Q_SCALE).astype(BF16)
    k_ref[0] = norm_rope(zr[:, ATT_WIDTH:ATT_WIDTH + KV_WIDTH], kg_ref[...], 1.0).astype(BF16)
    v_ref[0] = zr[:, ATT_WIDTH + KV_WIDTH:ATT_WIDTH + 2 * KV_WIDTH].astype(BF16)
    gate_ref[0] = jax.nn.sigmoid(zr[:, ATT_WIDTH + 2 * KV_WIDTH:])


def _rope_tables(seq):
    rows = seq // GRID_W
    t = np.arange(seq)
    row = (t // GRID_W).astype(np.float32)
    col = (t % GRID_W).astype(np.float32)
    half = HEAD_DIM // 2
    quarter = half // 2
    inv = (ROPE_THETA ** (-np.arange(0, half, 2, dtype=np.float32) / half)).astype(np.float32)
    ang_r = row[:, None] * inv[None, :]
    ang_c = col[:, None] * inv[None, :]
    ang = np.concatenate([ang_r, ang_r, ang_c, ang_c], axis=1)
    sign = np.concatenate([-np.ones(quarter), np.ones(quarter)] * 2).astype(np.float32)
    cos = np.cos(ang).astype(np.float32)
    sin = (np.sin(ang) * sign[None, :]).astype(np.float32)
    reps = LANES // HEAD_DIM
    del rows
    return jnp.asarray(np.tile(cos, (1, reps))), jnp.asarray(np.tile(sin, (1, reps)))


def _inproj(x, mod3, g_mix, w_in, b_in, q_gain, k_gain):
    bsz, seq, d = x.shape
    tm = TOKEN_TILE
    n_rest = w_in.shape[1] - HY_IN
    n_gate = n_rest - ATT_WIDTH - 2 * KV_WIDTH
    whyT = w_in[:, :HY_IN].T.astype(BF16)
    wr = w_in[:, HY_IN:].astype(BF16)
    bhy = b_in[:HY_IN].reshape(HY_IN, 1)
    br = b_in[HY_IN:].reshape(1, n_rest)
    cos, sin = _rope_tables(seq)
    reps = LANES // HEAD_DIM
    qg = jnp.tile(q_gain, reps).reshape(1, LANES)
    kg = jnp.tile(k_gain, reps).reshape(1, LANES)
    bd = jnp.asarray(np.kron(np.eye(reps, dtype=np.float32), np.ones((HEAD_DIM, HEAD_DIM), np.float32)))
    full = lambda shape: pl.BlockSpec(shape, lambda b, i: (0,) * len(shape))
    return pl.pallas_call(
        _inproj_kernel,
        grid=(bsz, seq // tm),
        in_specs=[pl.BlockSpec((1, tm, d), lambda b, i: (b, i, 0)),
                  pl.BlockSpec((1, 6, d), lambda b, i: (b, 0, 0)),
                  full((1, d)), full((HY_IN, d)), full((HY_IN, 1)), full((d, n_rest)), full((1, n_rest)),
                  pl.BlockSpec((tm, LANES), lambda b, i: (i, 0)),
                  pl.BlockSpec((tm, LANES), lambda b, i: (i, 0)),
                  full((1, LANES)), full((1, LANES)), full((LANES, LANES))],
        out_specs=[pl.BlockSpec((1, HY_IN, tm), lambda b, i: (b, 0, i)),
                   pl.BlockSpec((1, tm, ATT_WIDTH), lambda b, i: (b, i, 0)),
                   pl.BlockSpec((1, tm, KV_WIDTH), lambda b, i: (b, i, 0)),
                   pl.BlockSpec((1, tm, KV_WIDTH), lambda b, i: (b, i, 0)),
                   pl.BlockSpec((1, tm, n_gate), lambda b, i: (b, i, 0))],
        out_shape=[jax.ShapeDtypeStruct((bsz, HY_IN, seq), F32),
                   jax.ShapeDtypeStruct((bsz, seq, ATT_WIDTH), BF16),
                   jax.ShapeDtypeStruct((bsz, seq, KV_WIDTH), BF16),
                   jax.ShapeDtypeStruct((bsz, seq, KV_WIDTH), BF16),
                   jax.ShapeDtypeStruct((bsz, seq, n_gate), F32)],
        compiler_params=_params("parallel", "parallel"),
        name="inproj",
    )(x, mod3, g_mix.reshape(1, d), whyT, bhy, wr, br, cos, sin, qg, kg, bd)


def _filter_kernel(zT_ref, w1T_ref, b1_ref, w2T_ref, b2_ref, w3T_ref, b3_ref, fr_ref, w4T_ref,
                   t01_ref, absd_ref, o_ref, hdn_ref):
    @pl.when(pl.program_id(0) == 0)
    def _():
        fr = fr_ref[...]
        h = jnp.sin(fr[:, 0:1] * (jnp.dot(w1T_ref[...], zT_ref[...], precision=HIGHEST,
                                          preferred_element_type=F32) + b1_ref[...]))
        h = jnp.sin(fr[:, 1:2] * (jnp.dot(w2T_ref[...], h, precision=HIGHEST,
                                          preferred_element_type=F32) + b2_ref[...]))
        h = jnp.sin(fr[:, 2:3] * (jnp.dot(w3T_ref[...], h, precision=HIGHEST,
                                          preferred_element_type=F32) + b3_ref[...]))
        hdn_ref[...] = h

    hT = jnp.dot(w4T_ref[...], hdn_ref[...], precision=HIGHEST, preferred_element_type=F32)
    hT = hT * jnp.exp(-t01_ref[...] * absd_ref[...])
    nrm = jnp.sum(jnp.abs(hT), axis=1, keepdims=True) + EPS
    o_ref[...] = hT / nrm


def _filters(seq, w1, b1, w2, b2, w3, b3, w4, freq):
    emb_pad = 64
    bands = (FILTER_EMB - 1) // 2
    t01 = np.linspace(0.0, 1.0, seq, dtype=np.float32)[None, :]
    f = np.linspace(1e-4, bands - 1, bands, dtype=np.float32)[:, None]
    w = ((2.0 * math.pi) * np.arange(seq, dtype=np.float32) / seq).astype(np.float32)[None, :]
    zT = np.zeros((emb_pad, seq), np.float32)
    zT[0:1] = t01
    zT[1:1 + bands] = np.cos(f * w)
    zT[1 + bands:1 + 2 * bands] = -np.sin(f * w)
    w1T = jnp.zeros((FILTER_HIDDEN, emb_pad), F32).at[:, :FILTER_EMB].set(w1.T)
    max_decay = math.log(HY_DECAY_TARGET) / HY_QUICK_DECAY_PCT
    min_decay = math.log(HY_DECAY_TARGET) / HY_GRADUAL_DECAY_PCT
    deltas = np.abs(np.linspace(min_decay, max_decay, HY_WIDTH, dtype=np.float32))
    n_rows = HY_ORDER * 2 * HY_WIDTH
    absd = np.tile(deltas, HY_ORDER * 2).reshape(n_rows, 1)
    rb = FILTER_ROWS
    full = lambda shape: pl.BlockSpec(shape, lambda j: (0,) * len(shape))
    col = lambda v: v.reshape(FILTER_HIDDEN, 1)
    return pl.pallas_call(
        _filter_kernel,
        grid=(n_rows // rb,),
        in_specs=[full((emb_pad, seq)),
                  full((FILTER_HIDDEN, emb_pad)), full((FILTER_HIDDEN, 1)),
                  full((FILTER_HIDDEN, FILTER_HIDDEN)), full((FILTER_HIDDEN, 1)),
                  full((FILTER_HIDDEN, FILTER_HIDDEN)), full((FILTER_HIDDEN, 1)),
                  full((FILTER_HIDDEN, 3)),
                  pl.BlockSpec((rb, FILTER_HIDDEN), lambda j: (j, 0)),
                  full((1, seq)),
                  pl.BlockSpec((rb, 1), lambda j: (j, 0))],
        out_specs=pl.BlockSpec((rb, seq), lambda j: (j, 0)),
        out_shape=jax.ShapeDtypeStruct((n_rows, seq), F32),
        scratch_shapes=[pltpu.VMEM((FILTER_HIDDEN, seq), F32)],
        compiler_params=_params("arbitrary"),
        name="filters",
    )(jnp.asarray(zT), w1T, col(b1), w2.T, col(b2), w3.T, col(b3), freq.T, w4.T,
      jnp.asarray(t01), jnp.asarray(absd))


def _dft_tables(seq):
    n = 2 * seq
    r = int(round(math.sqrt(n)))
    assert r * r == n, "2*seq must be a perfect square"
    hh = r // 2
    k = np.arange(r, dtype=np.float64)
    ang = 2.0 * np.pi * np.outer(k, k) / r
    fr, fi = np.cos(ang), -np.sin(ang)
    angt = 2.0 * np.pi * np.outer(k, k) / n
    twr, twi = np.cos(angt), -np.sin(angt)
    f1 = np.concatenate([fr[:, :hh], fi[:, :hh]], axis=0)
    f3 = np.block([[fr, fi], [-fi, fr]])
    f3i = np.block([[fr, -fi], [fi, fr]])
    f1i = np.concatenate([fr[:hh, :], fi[:hh, :]], axis=1) / n
    as32 = lambda a: jnp.asarray(a.astype(np.float32))
    return as32(f1), as32(twr), as32(twi), as32(f3), as32(f3i), as32(f1i)


def _hyena_kernel(sw_ref, sb_ref, fb_ref, v_ref, x1_ref, x2_ref, h_ref, f1_f32, twr_ref, twi_ref,
                  f3_f32, f3i_f32, f1i_f32, o_ref, f1_ref, f3_ref, f3i_ref, f1i_ref):
    bsz, cb, hh, r = v_ref.shape
    j = pl.program_id(0)
    f1_ref[...] = f1_f32[...].astype(BF16)
    f3_ref[...] = f3_f32[...].astype(BF16)
    f3i_ref[...] = f3i_f32[...].astype(BF16)
    f1i_ref[...] = f1i_f32[...].astype(BF16)
    twr = twr_ref[...]
    twi = twi_ref[...]
    lane = lax.broadcasted_iota(I32, (hh, r), 1)
    row = lax.broadcasted_iota(I32, (hh, r), 0)

    def shift_prev(z):
        a = pltpu.roll(z, 1, 1)
        a = jnp.where(lane == 0, pltpu.roll(a, 1, 0), a)
        return jnp.where((lane == 0) & (row == 0), 0.0, a)

    def shift_next(z):
        a = pltpu.roll(z, r - 1, 1)
        a = jnp.where(lane == r - 1, pltpu.roll(a, hh - 1, 0), a)
        return jnp.where((lane == r - 1) & (row == hh - 1), 0.0, a)

    def fwd(slabs):
        m = len(slabs)
        acat = jnp.concatenate([a.astype(BF16) for a in slabs], axis=1)
        s = jnp.dot(f1_ref[...], acat, preferred_element_type=F32)
        lhs = []
        for i in range(m):
            br = s[:r, i * r:(i + 1) * r]
            bi = s[r:, i * r:(i + 1) * r]
            cr = br * twr - bi * twi
            ci = br * twi + bi * twr
            lhs.append(jnp.concatenate([cr, ci], axis=1).astype(BF16))
        d = jnp.dot(jnp.concatenate(lhs, axis=0), f3_ref[...], preferred_element_type=F32)
        return [(d[i * r:(i + 1) * r, :r], d[i * r:(i + 1) * r, r:]) for i in range(m)]

    def inv(specs):
        m = len(specs)
        lhs = jnp.concatenate([jnp.concatenate([pr, pi], axis=1).astype(BF16) for pr, pi in specs], axis=0)
        e = jnp.dot(lhs, f3i_ref[...], preferred_element_type=F32)
        rhs = []
        for i in range(m):
            er = e[i * r:(i + 1) * r, :r]
            ei = e[i * r:(i + 1) * r, r:]
            tr = er * twr + ei * twi
            ti = ei * twr - er * twi
            rhs.append(jnp.concatenate([tr, ti], axis=0).astype(BF16))
        y = jnp.dot(f1i_ref[...], jnp.concatenate(rhs, axis=1), preferred_element_type=F32)
        return [y[:, i * r:(i + 1) * r] for i in range(m)]

    def cmul(a, g):
        return a[0] * g[0] - a[1] * g[1], a[0] * g[1] + a[1] * g[0]

    def body(c, carry):
        ch = j * cb + c

        def sconv(ref, b, gi):
            z = ref[b, c]
            return (sb_ref[gi] + sw_ref[gi] * shift_prev(z) + sw_ref[HY_IN + gi] * z
                    + sw_ref[2 * HY_IN + gi] * shift_next(z))

        v = [sconv(v_ref, b, ch) for b in range(bsz)]
        filt = [h_ref[o, d, c] for o in range(HY_ORDER) for d in range(2)]
        spec = fwd(v + filt)
        g = []
        for o in range(HY_ORDER):
            hf = spec[bsz + 2 * o]
            hb = spec[bsz + 2 * o + 1]
            g.append((hf[0] + hb[0], hf[1] - hb[1]))
        conv = inv([cmul(spec[b], g[0]) for b in range(bsz)])
        y = [sconv(x1_ref, b, HY_WIDTH + ch) * (conv[b] + fb_ref[ch] * v[b]) for b in range(bsz)]
        spec = fwd(y)
        conv = inv([cmul(spec[b], g[1]) for b in range(bsz)])
        for b in range(bsz):
            o_ref[b, c] = sconv(x2_ref, b, 2 * HY_WIDTH + ch) * (conv[b] + fb_ref[HY_WIDTH + ch] * y[b])
        return carry

    lax.fori_loop(0, cb, body, 0)


def _hyena(zhyT, hT, short_w, short_b, filt_bias):
    bsz, _, seq = zhyT.shape
    tabs = _dft_tables(seq)
    r = tabs[1].shape[0]
    hh = r // 2
    z4 = zhyT.reshape(bsz, HY_IN, hh, r)
    h5 = hT.reshape(HY_ORDER, 2, HY_WIDTH, hh, r)
    cb = HY_CH_BLOCK
    nblk = HY_WIDTH // cb
    smem = pl.BlockSpec(memory_space=pltpu.SMEM)
    full = lambda a: pl.BlockSpec(a.shape, lambda j: (0,) * a.ndim)
    zspec = lambda off: pl.BlockSpec((bsz, cb, hh, r), lambda j: (0, off * nblk + j, 0, 0))
    out = pl.pallas_call(
        _hyena_kernel,
        grid=(nblk,),
        in_specs=[smem, smem, smem, zspec(0), zspec(1), zspec(2),
                  pl.BlockSpec((HY_ORDER, 2, cb, hh, r), lambda j: (0, 0, j, 0, 0))]
                 + [full(t) for t in tabs],
        out_specs=pl.BlockSpec((bsz, cb, hh, r), lambda j: (0, j, 0, 0)),
        out_shape=jax.ShapeDtypeStruct((bsz, HY_WIDTH, hh, r), F32),
        scratch_shapes=[pltpu.VMEM(tabs[i].shape, BF16) for i in (0, 3, 4, 5)],
        compiler_params=_params("parallel"),
        name="hyena",
    )(short_w.reshape(-1), short_b, filt_bias.reshape(-1), z4, z4, z4, h5, *tabs)
    return out.reshape(bsz, HY_WIDTH, seq)


def _attn_kernel(q_ref, k_ref, vt_ref, o_ref):
    g, tq, hd = q_ref.shape[1:]
    seq = k_ref.shape[2]
    rows = vt_ref.shape[2]
    q = q_ref[0].reshape(g * tq, hd)

    def scores(kc):
        kb = k_ref[0, 0, kc * KV_CHUNK:(kc + 1) * KV_CHUNK, :]
        return lax.dot_general(kb, q, NT_DIMS, preferred_element_type=F32)

    n_chunks = seq // KV_CHUNK
    m = jnp.full((1, g * tq), -jnp.inf, F32)
    acc = jnp.zeros((rows, g * tq), F32)
    s_next = scores(0)
    for kc in range(n_chunks):
        s = s_next
        if kc + 1 < n_chunks:
            s_next = scores(kc + 1)
        vtb = vt_ref[0, 0, :, kc * KV_CHUNK:(kc + 1) * KV_CHUNK]
        m_new = jnp.maximum(m, jnp.max(s, axis=0, keepdims=True))
        p = jnp.exp2(s - m_new).astype(BF16)
        acc = jnp.exp2(m - m_new) * acc + jnp.dot(vtb, p, preferred_element_type=F32)
        m = m_new
    out = acc[:hd] / acc[hd:hd + 1]
    o_ref[0] = out.T.reshape(g, tq, hd).astype(o_ref.dtype)


def _attention(q, k, v):
    bsz, seq, _ = q.shape
    g = N_HEADS // N_KV_HEADS
    qh = q.reshape(bsz, seq, N_HEADS, HEAD_DIM).transpose(0, 2, 1, 3)
    kh = k.reshape(bsz, seq, N_KV_HEADS, HEAD_DIM).transpose(0, 2, 1, 3)
    vt = v.reshape(bsz, seq, N_KV_HEADS, HEAD_DIM).transpose(0, 2, 3, 1)
    vt = jnp.concatenate([vt, jnp.ones((bsz, N_KV_HEADS, ONES_ROWS, seq), BF16)], axis=2)
    rows = HEAD_DIM + ONES_ROWS
    tq = Q_TILE
    o = pl.pallas_call(
        _attn_kernel,
        grid=(bsz, N_KV_HEADS, seq // tq),
        in_specs=[pl.BlockSpec((1, g, tq, HEAD_DIM), lambda b, h, i: (b, h, i, 0)),
                  pl.BlockSpec((1, 1, seq, HEAD_DIM), lambda b, h, i: (b, h, 0, 0)),
                  pl.BlockSpec((1, 1, rows, seq), lambda b, h, i: (b, h, 0, 0))],
        out_specs=pl.BlockSpec((1, g, tq, HEAD_DIM), lambda b, h, i: (b, h, i, 0)),
        out_shape=jax.ShapeDtypeStruct((bsz, N_HEADS, seq, HEAD_DIM), BF16),
        compiler_params=_params("parallel", "parallel", "parallel"),
        name="attention",
    )(qh, kh, vt)
    return o.transpose(0, 2, 1, 3).reshape(bsz, seq, ATT_WIDTH)


def _merge_kernel(x_ref, yhyT_ref, yat_ref, gate_ref, mod_ref, g_ref, why_ref, wat_ref, wout_ref, wrT_ref,
                  x1_ref, h2_ref, affT_ref):
    d = x_ref.shape[2]
    yhy = yhyT_ref[0].T.astype(BF16)
    a = jnp.dot(yhy, why_ref[...], preferred_element_type=F32)
    b = jnp.dot(yat_ref[0], wat_ref[...], preferred_element_type=F32)
    gate = gate_ref[0]
    mrg = gate[:, :d] * a + gate[:, d:] * b
    o = jnp.dot(mrg.astype(BF16), wout_ref[...], preferred_element_type=F32)
    gt1 = mod_ref[0, 2:3, :]
    sh2 = mod_ref[0, 3:4, :]
    sc2 = mod_ref[0, 4:5, :]
    x1 = x_ref[0] + gt1 * o
    x1_ref[0] = x1
    ms = jnp.mean(x1 * x1, axis=-1, keepdims=True)
    h2 = (x1 * lax.rsqrt(ms + EPS)) * g_ref[...] * (1.0 + sc2) + sh2
    h2_ref[0] = h2.astype(BF16)
    logT = lax.dot_general(wrT_ref[...], h2, NT_DIMS, precision=HIGHEST, preferred_element_type=F32)
    mx = jnp.max(logT, axis=0, keepdims=True)
    ex = jnp.exp(logT - mx)
    affT_ref[0] = ex / jnp.sum(ex, axis=0, keepdims=True)


def _merge(x, yhyT, yat, gates, mod3, g_ffn, w_hy_out, w_att_out, w_out, w_router):
    bsz, seq, d = x.shape
    tm = TOKEN_TILE
    ne = w_router.shape[1]
    full = lambda shape: pl.BlockSpec(shape, lambda b, i: (0,) * len(shape))
    return pl.pallas_call(
        _merge_kernel,
        grid=(bsz, seq // tm),
        in_specs=[pl.BlockSpec((1, tm, d), lambda b, i: (b, i, 0)),
                  pl.BlockSpec((1, HY_WIDTH, tm), lambda b, i: (b, 0, i)),
                  pl.BlockSpec((1, tm, ATT_WIDTH), lambda b, i: (b, i, 0)),
                  pl.BlockSpec((1, tm, 2 * d), lambda b, i: (b, i, 0)),
                  pl.BlockSpec((1, 6, d), lambda b, i: (b, 0, 0)),
                  full((1, d)), full((HY_WIDTH, d)), full((ATT_WIDTH, d)), full((d, d)), full((ne, d))],
        out_specs=[pl.BlockSpec((1, tm, d), lambda b, i: (b, i, 0)),
                   pl.BlockSpec((1, tm, d), lambda b, i: (b, i, 0)),
                   pl.BlockSpec((1, ne, tm), lambda b, i: (b, 0, i))],
        out_shape=[jax.ShapeDtypeStruct((bsz, seq, d), F32),
                   jax.ShapeDtypeStruct((bsz, seq, d), BF16),
                   jax.ShapeDtypeStruct((bsz, ne, seq), F32)],
        compiler_params=_params("parallel", "parallel"),
        name="merge",
    )(x, yhyT, yat, gates, mod3, g_ffn.reshape(1, d), w_hy_out.astype(BF16), w_att_out.astype(BF16),
      w_out.astype(BF16), w_router.T)


def _topk_kernel(a_ref, ut_ref, pos_ref, *, cap):
    nr, seq = a_ref.shape
    a = a_ref[...]

    def count(mask):
        return jnp.sum(mask.astype(F32), axis=1, keepdims=True)

    def probe(t, lo, hi):
        ok = count(a >= t) >= cap
        return ok, jnp.where(ok, t, lo), jnp.where(ok, hi, t)

    def by_exponent(_, c):
        elo, ehi, lo, hi = c
        emid = 0.5 * (elo + ehi)
        ok, lo, hi = probe(jnp.exp2(emid), lo, hi)
        return jnp.where(ok, emid, elo), jnp.where(ok, ehi, emid), lo, hi

    def by_value(_, c):
        lo, hi = c
        _, lo, hi = probe(0.5 * (lo + hi), lo, hi)
        return lo, hi

    col = lambda v: jnp.full((nr, 1), v, F32)
    _, _, lo, hi = lax.fori_loop(0, TOPK_EXP_STEPS, by_exponent, (col(F32_MIN_EXP), col(1.0), col(0.0), col(2.0)))
    lo, hi = lax.fori_loop(0, TOPK_VALUE_STEPS, by_value, (lo, hi))
    gt = a >= hi
    eq = (a >= lo) & (a < hi)
    need = cap - count(gt)
    ut = ut_ref[...]
    run_eq = jnp.zeros((nr, 1), F32)
    run_sel = jnp.zeros((nr, 1), F32)
    for ch in range(seq // LANES):
        sl = slice(ch * LANES, (ch + 1) * LANES)
        eq_rank = jnp.dot(eq[:, sl].astype(BF16), ut, preferred_element_type=F32) + run_eq
        run_eq = eq_rank[:, LANES - 1:LANES]
        sel = gt[:, sl] | (eq[:, sl] & (eq_rank <= need))
        pos = jnp.dot(sel.astype(BF16), ut, preferred_element_type=F32) + run_sel
        run_sel = pos[:, LANES - 1:LANES]
        pos_ref[:, sl] = jnp.where(sel, pos.astype(I32) - 1, -1)


def _topk(aff2, cap):
    nr, seq = aff2.shape
    ut = jnp.asarray(np.triu(np.ones((LANES, LANES), np.float32))).astype(BF16)
    return pl.pallas_call(
        functools.partial(_topk_kernel, cap=cap),
        grid=(1,),
        in_specs=[pl.BlockSpec((nr, seq), lambda i: (0, 0)), pl.BlockSpec((LANES, LANES), lambda i: (0, 0))],
        out_specs=pl.BlockSpec((nr, seq), lambda i: (0, 0)),
        out_shape=jax.ShapeDtypeStruct((nr, seq), I32),
        compiler_params=_params("arbitrary"),
        name="topk",
    )(aff2, ut)


def _onehot(pos_row, r, tile):
    dest = lax.broadcasted_iota(I32, (tile, tile), 0) + r * tile
    return (pos_row == dest).astype(BF16)


def _moe_kernel(lo_ref, hi_ref, pos_ref, a_ref, h2_ref, wg_ref, wu_ref, wd_ref, ye_ref, xe_ref, w_ref, *, nblk):
    tile = TOKEN_TILE
    ne = pl.num_programs(1)
    base = (pl.program_id(0) * ne + pl.program_id(1)) * nblk
    for r in range(nblk):
        xe_ref[...] = jnp.zeros_like(xe_ref)
        w_ref[...] = jnp.zeros_like(w_ref)

        def body(i, carry):
            st = pl.multiple_of(i * tile, tile)
            p = _onehot(pos_ref[0, 0, :, pl.ds(st, tile)], r, tile)
            xe_ref[...] += jnp.dot(p, h2_ref[0, pl.ds(st, tile), :], preferred_element_type=F32)
            w_ref[...] += lax.dot_general(p, a_ref[0, 0, :, pl.ds(st, tile)], NT_DIMS,
                                          preferred_element_type=F32)
            return carry

        lax.fori_loop(lo_ref[base + r], hi_ref[base + r], body, 0)
        xe = xe_ref[...].astype(BF16)
        a = jnp.dot(xe, wg_ref[0], preferred_element_type=F32)
        u = jnp.dot(xe, wu_ref[0], preferred_element_type=F32)
        hmid = (a * jax.nn.sigmoid(a) * u).astype(BF16)
        ye = jnp.dot(hmid, wd_ref[0], preferred_element_type=F32)
        w = w_ref[:, 0:1] + w_ref[:, 1:2] + w_ref[:, 2:3]
        ye_ref[0, 0, r * tile:(r + 1) * tile, :] = (ye * w).astype(BF16)


def _moe(tile_lo, tile_hi, pos4, a4, h2, wg, wu, wd, cap):
    bsz, seq, d = h2.shape
    ne, _, dff = wg.shape
    nblk = cap // TOKEN_TILE
    grid_spec = pltpu.PrefetchScalarGridSpec(
        num_scalar_prefetch=2,
        grid=(bsz, ne),
        in_specs=[pl.BlockSpec((1, 1, 1, seq), lambda b, e, lo, hi: (b, e, 0, 0)),
                  pl.BlockSpec((1, 1, A_ROWS, seq), lambda b, e, lo, hi: (b, e, 0, 0)),
                  pl.BlockSpec((1, seq, d), lambda b, e, lo, hi: (b, 0, 0), pipeline_mode=pl.Buffered(1)),
                  pl.BlockSpec((1, d, dff), lambda b, e, lo, hi: (e, 0, 0)),
                  pl.BlockSpec((1, d, dff), lambda b, e, lo, hi: (e, 0, 0)),
                  pl.BlockSpec((1, dff, d), lambda b, e, lo, hi: (e, 0, 0))],
        out_specs=pl.BlockSpec((1, 1, cap, d), lambda b, e, lo, hi: (b, e, 0, 0)),
        scratch_shapes=[pltpu.VMEM((TOKEN_TILE, d), F32), pltpu.VMEM((TOKEN_TILE, A_ROWS), F32)])
    return pl.pallas_call(
        functools.partial(_moe_kernel, nblk=nblk),
        grid_spec=grid_spec,
        out_shape=jax.ShapeDtypeStruct((bsz, ne, cap, d), BF16),
        compiler_params=_params("parallel", "arbitrary"),
        name="moe",
    )(tile_lo, tile_hi, pos4, a4, h2, wg, wu, wd)


def _combine_kernel(lo_ref, hi_ref, pos_ref, ye_ref, x1_ref, mod_ref, o_ref, *, nblk):
    tile = TOKEN_TILE
    tiles_here = o_ref.shape[1] // tile
    ne = pl.num_programs(2)
    e = pl.program_id(2)
    first = pl.program_id(1) * tiles_here
    base = (pl.program_id(0) * ne + e) * nblk

    @pl.when(e == 0)
    def _():
        o_ref[...] = jnp.zeros_like(o_ref)

    for r in range(nblk):
        yer = ye_ref[0, 0, r * tile:(r + 1) * tile, :]

        def body(i, carry):
            src = pl.multiple_of(i * tile, tile)
            dst = pl.multiple_of((i - first) * tile, tile)
            p = _onehot(pos_ref[0, 0, :, pl.ds(src, tile)], r, tile)
            o_ref[0, pl.ds(dst, tile), :] += lax.dot_general(p, yer, TN_DIMS, preferred_element_type=F32)
            return carry

        lo = jnp.maximum(lo_ref[base + r], first)
        hi = jnp.minimum(hi_ref[base + r], first + tiles_here)
        lax.fori_loop(lo, hi, body, 0)

    @pl.when(e == ne - 1)
    def _():
        o_ref[0] = x1_ref[0] + mod_ref[0, 5:6, :] * o_ref[0]


def _combine(tile_lo, tile_hi, pos4, ye, x1, mod3, cap):
    bsz, seq, d = x1.shape
    ne = ye.shape[1]
    nblk = cap // TOKEN_TILE
    tb = seq // COMBINE_SPLIT
    grid_spec = pltpu.PrefetchScalarGridSpec(
        num_scalar_prefetch=2,
        grid=(bsz, COMBINE_SPLIT, ne),
        in_specs=[pl.BlockSpec((1, 1, 1, seq), lambda b, t, e, lo, hi: (b, e, 0, 0)),
                  pl.BlockSpec((1, 1, cap, d), lambda b, t, e, lo, hi: (b, e, 0, 0)),
                  pl.BlockSpec((1, tb, d), lambda b, t, e, lo, hi: (b, t, 0)),
                  pl.BlockSpec((1, 6, d), lambda b, t, e, lo, hi: (b, 0, 0))],
        out_specs=pl.BlockSpec((1, tb, d), lambda b, t, e, lo, hi: (b, t, 0)))
    return pl.pallas_call(
        functools.partial(_combine_kernel, nblk=nblk),
        grid_spec=grid_spec,
        out_shape=jax.ShapeDtypeStruct((bsz, seq, d), F32),
        compiler_params=_params("parallel", "parallel", "arbitrary"),
        name="combine",
    )(tile_lo, tile_hi, pos4, ye, x1, mod3)


def _route_tables(pos, cap):
    bsz, ne, seq = pos.shape
    tile = TOKEN_TILE
    counts = jnp.sum((pos >= 0).reshape(bsz, ne, seq // tile, tile), axis=-1, dtype=I32)
    ends = jnp.cumsum(counts, axis=-1)
    starts = ends - counts
    edges = jnp.arange(cap // tile, dtype=I32) * tile
    lo = jnp.sum(ends[..., None, :] <= edges[:, None], axis=-1, dtype=I32)
    hi = jnp.sum(starts[..., None, :] < (edges[:, None] + tile), axis=-1, dtype=I32)
    return lo.reshape(-1), hi.reshape(-1)


def _split_bf16(a, rows):
    hi = a.astype(BF16)
    r1 = a - hi.astype(F32)
    mid = r1.astype(BF16)
    lo = (r1 - mid.astype(F32)).astype(BF16)
    pad = jnp.zeros(a.shape[:-1] + (rows - 3, a.shape[-1]), BF16)
    return jnp.concatenate([hi[..., None, :], mid[..., None, :], lo[..., None, :], pad], axis=-2)


def kernel(x, c, w_ada, b_ada, g_mix, g_ffn, w_in, b_in, short_w, short_b, hy_w1, hy_b1, hy_w2, hy_b2, hy_w3, hy_b3, hy_w4, hy_freq, hy_bias, q_gain, k_gain, w_hy_out, w_att_out, w_out, w_router, w_gate, w_up, w_down):
    bsz, seq, d = x.shape
    depth = w_ada.shape[0]
    ne = w_router.shape[-1]
    cap = EC_FACTOR * seq // ne
    for l in range(depth):
        mod3 = _adaln(c, w_ada[l], b_ada[l]).reshape(bsz, 6, d)
        zhyT, q, k, v, gates = _inproj(x, mod3, g_mix[l], w_in[l], b_in[l], q_gain[l], k_gain[l])
        hT = _filters(seq, hy_w1[l], hy_b1[l], hy_w2[l], hy_b2[l], hy_w3[l], hy_b3[l], hy_w4[l], hy_freq[l])
        yhyT = _hyena(zhyT, hT, short_w[l], short_b[l], hy_bias[l])
        yat = _attention(q, k, v)
        x1, h2, affT = _merge(x, yhyT, yat, gates, mod3, g_ffn[l], w_hy_out[l], w_att_out[l], w_out[l],
                              w_router[l])
        pos = _topk(affT.reshape(bsz * ne, seq), cap).reshape(bsz, ne, seq)
        tile_lo, tile_hi = _route_tables(pos, cap)
        pos4 = pos.reshape(bsz, ne, 1, seq)
        ye = _moe(tile_lo, tile_hi, pos4, _split_bf16(affT, A_ROWS), h2,
                  w_gate[l].astype(BF16), w_up[l].astype(BF16), w_down[l].astype(BF16), cap)
        x = _combine(tile_lo, tile_hi, pos4, ye, x1, mod3, cap)
    return x
```

```python
import functools
import math

import numpy as np
import jax
import jax.numpy as jnp
from jax import lax
from jax.experimental import pallas as pl
from jax.experimental.pallas import tpu as pltpu

F32 = jnp.float32
BF16 = jnp.bfloat16
I32 = jnp.int32
HIGHEST = lax.Precision.HIGHEST

EPS = 1e-6
GRID_W = 64
HY_WIDTH = 512
HY_ORDER = 2
SHORT_K = 3
FILTER_EMB = 33
FILTER_HIDDEN = 64
HY_QUICK_DECAY_PCT = 0.3
HY_GRADUAL_DECAY_PCT = 1.5
HY_DECAY_TARGET = 1e-2
N_HEADS = 8
N_KV_HEADS = 2
HEAD_DIM = 64
ROPE_THETA = 10000.0
N_EXPERTS = 16
EC_FACTOR = 2
HY_IN = 3 * HY_WIDTH
ATT_WIDTH = N_HEADS * HEAD_DIM
KV_WIDTH = N_KV_HEADS * HEAD_DIM

LANES = 128
VMEM_LIMIT = 56 * 1024 * 1024
MXU_TILE = 256
TOKEN_TILE = 256
Q_TILE = 128
KV_CHUNK = 1024
HY_CH_BLOCK = 8
HY_CH_BATCH = 4
FILTER_ROWS = 128
A_ROWS = 16
SAFE_SCORE_BOUND = 48.0
BF16_NORM_SLACK = 1.01
F32_MIN_EXP = -150.0
TOPK_EXP_STEPS = 10
TOPK_VALUE_STEPS = 28
COMBINE_SPLIT = 4
ONES_ROWS = 16
Q_SCALE = HEAD_DIM ** -0.5 * math.log2(math.e)

NT_DIMS = (((1,), (1,)), ((), ()))
TN_DIMS = (((0,), (0,)), ((), ()))


def _params(*sem):
    return pltpu.CompilerParams(dimension_semantics=sem, vmem_limit_bytes=VMEM_LIMIT)


def _adaln_kernel(c_ref, w_ref, b_ref, o_ref):
    c = c_ref[...]
    s = c * jax.nn.sigmoid(c)
    o_ref[...] = jnp.dot(s, w_ref[...], precision=HIGHEST, preferred_element_type=F32) + b_ref[...]


def _adaln(c, w, b):
    bsz, d = c.shape
    n = w.shape[1]
    rows = 8
    cp = jnp.zeros((rows, d), F32).at[:bsz].set(c)
    tn = 1536
    out = pl.pallas_call(
        _adaln_kernel,
        grid=(n // tn,),
        in_specs=[pl.BlockSpec((rows, d), lambda j: (0, 0)),
                  pl.BlockSpec((d, tn), lambda j: (0, j)),
                  pl.BlockSpec((1, tn), lambda j: (0, j))],
        out_specs=pl.BlockSpec((rows, tn), lambda j: (0, j)),
        out_shape=jax.ShapeDtypeStruct((rows, n), F32),
        compiler_params=_params("parallel"),
        name="adaln",
    )(cp, w, b.reshape(1, n))
    return out[:bsz]


def _inproj_kernel(x_ref, mod_ref, g_ref, whyT_ref, bhy_ref, wr_ref, br_ref, cos_ref, sin_ref,
                   qg_ref, kg_ref, bd_ref, zhyT_ref, q_ref, k_ref, v_ref, gate_ref):
    x = x_ref[0]
    ms = jnp.mean(x * x, axis=-1, keepdims=True)
    sh1 = mod_ref[0, 0:1, :]
    sc1 = mod_ref[0, 1:2, :]
    h = (x * lax.rsqrt(ms + EPS)) * g_ref[...] * (1.0 + sc1) + sh1
    hb = h.astype(BF16)
    zhyT_ref[0] = lax.dot_general(whyT_ref[...], hb, NT_DIMS, preferred_element_type=F32) + bhy_ref[...]
    zr = jnp.dot(hb, wr_ref[...], preferred_element_type=F32) + br_ref[...]

    cos = cos_ref[...]
    sin = sin_ref[...]
    bd = bd_ref[...]
    lane = lax.broadcasted_iota(I32, cos.shape, 1)
    first = (lane & (HEAD_DIM // 2 - 1)) < HEAD_DIM // 4

    def norm_rope(u, gain, scale):
        ss = jnp.dot(u * u, bd, precision=HIGHEST, preferred_element_type=F32)
        un = (u * lax.rsqrt(ss * (1.0 / HEAD_DIM) + EPS)) * gain
        rot = jnp.where(first, pltpu.roll(un, LANES - HEAD_DIM // 4, 1), pltpu.roll(un, HEAD_DIM // 4, 1))
        return (un * cos + rot * sin) * scale

    for j in range(ATT_WIDTH // LANES):
        u = zr[:, j * LANES:(j + 1) * LANES]
        q_ref[0, :, j * LANES:(j + 1) * LANES] = norm_rope(u, qg_ref[...], Q_SCALE).astype(BF16)
    k_ref[0] = norm_rope(zr[:, ATT_WIDTH:ATT_WIDTH + KV_WIDTH], kg_ref[...], 1.0).astype(BF16)
    v_ref[0] = zr[:, ATT_WIDTH + KV_WIDTH:ATT_WIDTH + 2 * KV_WIDTH].astype(BF16)
    gate_ref[0] = jax.nn.sigmoid(zr[:, ATT_WIDTH + 2 * KV_WIDTH:])


def _rope_tables(seq):
    rows = seq // GRID_W
    t = np.arange(seq)
    row = (t // GRID_W).astype(np.float32)
    col = (t % GRID_W).astype(np.float32)
    half = HEAD_DIM // 2
    quarter = half // 2
    inv = (ROPE_THETA ** (-np.arange(0, half, 2, dtype=np.float32) / half)).astype(np.float32)
    ang_r = row[:, None] * inv[None, :]
    ang_c = col[:, None] * inv[None, :]
    ang = np.concatenate([ang_r, ang_r, ang_c, ang_c], axis=1)
    sign = np.concatenate([-np.ones(quarter), np.ones(quarter)] * 2).astype(np.float32)
    cos = np.cos(ang).astype(np.float32)
    sin = (np.sin(ang) * sign[None, :]).astype(np.float32)
    reps = LANES // HEAD_DIM
    del rows
    return jnp.asarray(np.tile(cos, (1, reps))), jnp.asarray(np.tile(sin, (1, reps)))


def _inproj(x, mod3, g_mix, w_in, b_in, q_gain, k_gain):
    bsz, seq, d = x.shape
    tm = TOKEN_TILE
    n_rest = w_in.shape[1] - HY_IN
    n_gate = n_rest - ATT_WIDTH - 2 * KV_WIDTH
    whyT = w_in[:, :HY_IN].T.astype(BF16)
    wr = w_in[:, HY_IN:].astype(BF16)
    bhy = b_in[:HY_IN].reshape(HY_IN, 1)
    br = b_in[HY_IN:].reshape(1, n_rest)
    cos, sin = _rope_tables(seq)
    reps = LANES // HEAD_DIM
    qg = jnp.tile(q_gain, reps).reshape(1, LANES)
    kg = jnp.tile(k_gain, reps).reshape(1, LANES)
    bd = jnp.asarray(np.kron(np.eye(reps, dtype=np.float32), np.ones((HEAD_DIM, HEAD_DIM), np.float32)))
    full = lambda shape: pl.BlockSpec(shape, lambda b, i: (0,) * len(shape))
    return pl.pallas_call(
        _inproj_kernel,
        grid=(bsz, seq // tm),
        in_specs=[pl.BlockSpec((1, tm, d), lambda b, i: (b, i, 0)),
                  pl.BlockSpec((1, 6, d), lambda b, i: (b, 0, 0)),
                  full((1, d)), full((HY_IN, d)), full((HY_IN, 1)), full((d, n_rest)), full((1, n_rest)),
                  pl.BlockSpec((tm, LANES), lambda b, i: (i, 0)),
                  pl.BlockSpec((tm, LANES), lambda b, i: (i, 0)),
                  full((1, LANES)), full((1, LANES)), full((LANES, LANES))],
        out_specs=[pl.BlockSpec((1, HY_IN, tm), lambda b, i: (b, 0, i)),
                   pl.BlockSpec((1, tm, ATT_WIDTH), lambda b, i: (b, i, 0)),
                   pl.BlockSpec((1, tm, KV_WIDTH), lambda b, i: (b, i, 0)),
                   pl.BlockSpec((1, tm, KV_WIDTH), lambda b, i: (b, i, 0)),
                   pl.BlockSpec((1, tm, n_gate), lambda b, i: (b, i, 0))],
        out_shape=[jax.ShapeDtypeStruct((bsz, HY_IN, seq), F32),
                   jax.ShapeDtypeStruct((bsz, seq, ATT_WIDTH), BF16),
                   jax.ShapeDtypeStruct((bsz, seq, KV_WIDTH), BF16),
                   jax.ShapeDtypeStruct((bsz, seq, KV_WIDTH), BF16),
                   jax.ShapeDtypeStruct((bsz, seq, n_gate), F32)],
        compiler_params=_params("parallel", "parallel"),
        name="inproj",
    )(x, mod3, g_mix.reshape(1, d), whyT, bhy, wr, br, cos, sin, qg, kg, bd)


def _filter_kernel(zT_ref, w1T_ref, b1_ref, w2T_ref, b2_ref, w3T_ref, b3_ref, fr_ref, w4T_ref,
                   t01_ref, absd_ref, o_ref, hdn_ref):
    @pl.when(pl.program_id(0) == 0)
    def _():
        fr = fr_ref[...]
        h = jnp.sin(fr[:, 0:1] * (jnp.dot(w1T_ref[...], zT_ref[...], precision=HIGHEST,
                                          preferred_element_type=F32) + b1_ref[...]))
        h = jnp.sin(fr[:, 1:2] * (jnp.dot(w2T_ref[...], h, precision=HIGHEST,
                                          preferred_element_type=F32) + b2_ref[...]))
        h = jnp.sin(fr[:, 2:3] * (jnp.dot(w3T_ref[...], h, precision=HIGHEST,
                                          preferred_element_type=F32) + b3_ref[...]))
        hdn_ref[...] = h

    hT = jnp.dot(w4T_ref[...], hdn_ref[...], precision=HIGHEST, preferred_element_type=F32)
    hT = hT * jnp.exp(-t01_ref[...] * absd_ref[...])
    nrm = jnp.sum(jnp.abs(hT), axis=1, keepdims=True) + EPS
    o_ref[...] = hT / nrm


def _filters(seq, w1, b1, w2, b2, w3, b3, w4, freq):
    emb_pad = 64
    bands = (FILTER_EMB - 1) // 2
    t01 = np.linspace(0.0, 1.0, seq, dtype=np.float32)[None, :]
    f = np.linspace(1e-4, bands - 1, bands, dtype=np.float32)[:, None]
    w = ((2.0 * math.pi) * np.arange(seq, dtype=np.float32) / seq).astype(np.float32)[None, :]
    zT = np.zeros((emb_pad, seq), np.float32)
    zT[0:1] = t01
    zT[1:1 + bands] = np.cos(f * w)
    zT[1 + bands:1 + 2 * bands] = -np.sin(f * w)
    w1T = jnp.zeros((FILTER_HIDDEN, emb_pad), F32).at[:, :FILTER_EMB].set(w1.T)
    max_decay = math.log(HY_DECAY_TARGET) / HY_QUICK_DECAY_PCT
    min_decay = math.log(HY_DECAY_TARGET) / HY_GRADUAL_DECAY_PCT
    deltas = np.abs(np.linspace(min_decay, max_decay, HY_WIDTH, dtype=np.float32))
    n_rows = HY_ORDER * 2 * HY_WIDTH
    absd = np.tile(deltas, HY_ORDER * 2).reshape(n_rows, 1)
    rb = FILTER_ROWS
    full = lambda shape: pl.BlockSpec(shape, lambda j: (0,) * len(shape))
    col = lambda v: v.reshape(FILTER_HIDDEN, 1)
    return pl.pallas_call(
        _filter_kernel,
        grid=(n_rows // rb,),
        in_specs=[full((emb_pad, seq)),
                  full((FILTER_HIDDEN, emb_pad)), full((FILTER_HIDDEN, 1)),
                  full((FILTER_HIDDEN, FILTER_HIDDEN)), full((FILTER_HIDDEN, 1)),
                  full((FILTER_HIDDEN, FILTER_HIDDEN)), full((FILTER_HIDDEN, 1)),
                  full((FILTER_HIDDEN, 3)),
                  pl.BlockSpec((rb, FILTER_HIDDEN), lambda j: (j, 0)),
                  full((1, seq)),
                  pl.BlockSpec((rb, 1), lambda j: (j, 0))],
        out_specs=pl.BlockSpec((rb, seq), lambda j: (j, 0)),
        out_shape=jax.ShapeDtypeStruct((n_rows, seq), F32),
        scratch_shapes=[pltpu.VMEM((FILTER_HIDDEN, seq), F32)],
        compiler_params=_params("arbitrary"),
        name="filters",
    )(jnp.asarray(zT), w1T, col(b1), w2.T, col(b2), w3.T, col(b3), freq.T, w4.T,
      jnp.asarray(t01), jnp.asarray(absd))


def _dft_tables(seq):
    n = 2 * seq
    r = int(round(math.sqrt(n)))
    assert r * r == n, "2*seq must be a perfect square"
    hh = r // 2
    k = np.arange(r, dtype=np.float64)
    ang = 2.0 * np.pi * np.outer(k, k) / r
    fr, fi = np.cos(ang), -np.sin(ang)
    angt = 2.0 * np.pi * np.outer(k, k) / n
    twr, twi = np.cos(angt), -np.sin(angt)
    f1 = np.concatenate([fr[:, :hh], fi[:, :hh]], axis=0)
    f3 = np.block([[fr, fi], [-fi, fr]])
    f3i = np.block([[fr, -fi], [fi, fr]])
    f1i = np.concatenate([fr[:hh, :], fi[:hh, :]], axis=1) / n
    as32 = lambda a: jnp.asarray(a.astype(np.float32))
    return as32(f1), as32(twr), as32(twi), as32(f3), as32(f3i), as32(f1i)


def _hyena_kernel(sw_ref, sb_ref, fb_ref, v_ref, x1_ref, x2_ref, h_ref, f1_f32, twr_ref, twi_ref,
                  f3_f32, f3i_f32, f1i_f32, o_ref, f1_ref, f3_ref, f3i_ref, f1i_ref):
    bsz, cb, hh, r = v_ref.shape
    j = pl.program_id(0)
    f1_ref[...] = f1_f32[...].astype(BF16)
    f3_ref[...] = f3_f32[...].astype(BF16)
    f3i_ref[...] = f3i_f32[...].astype(BF16)
    f1i_ref[...] = f1i_f32[...].astype(BF16)
    twr = twr_ref[...]
    twi = twi_ref[...]
    lane = lax.broadcasted_iota(I32, (hh, r), 1)
    row = lax.broadcasted_iota(I32, (hh, r), 0)

    def shift_prev(z):
        a = pltpu.roll(z, 1, 1)
        a = jnp.where(lane == 0, pltpu.roll(a, 1, 0), a)
        return jnp.where((lane == 0) & (row == 0), 0.0, a)

    def shift_next(z):
        a = pltpu.roll(z, r - 1, 1)
        a = jnp.where(lane == r - 1, pltpu.roll(a, hh - 1, 0), a)
        return jnp.where((lane == r - 1) & (row == hh - 1), 0.0, a)

    def fwd(slabs):
        m = len(slabs)
        acat = jnp.concatenate([a.astype(BF16) for a in slabs], axis=1)
        s = jnp.dot(f1_ref[...], acat, preferred_element_type=F32)
        lhs = []
        for i in range(m):
            br = s[:r, i * r:(i + 1) * r]
            bi = s[r:, i * r:(i + 1) * r]
            cr = br * twr - bi * twi
            ci = br * twi + bi * twr
            lhs.append(jnp.concatenate([cr, ci], axis=1).astype(BF16))
        d = jnp.dot(jnp.concatenate(lhs, axis=0), f3_ref[...], preferred_element_type=F32)
        return [(d[i * r:(i + 1) * r, :r], d[i * r:(i + 1) * r, r:]) for i in range(m)]

    def inv(specs):
        m = len(specs)
        lhs = jnp.concatenate([jnp.concatenate([pr, pi], axis=1).astype(BF16) for pr, pi in specs], axis=0)
        e = jnp.dot(lhs, f3i_ref[...], preferred_element_type=F32)
        rhs = []
        for i in range(m):
            er = e[i * r:(i + 1) * r, :r]
            ei = e[i * r:(i + 1) * r, r:]
            tr = er * twr + ei * twi
            ti = ei * twr - er * twi
            rhs.append(jnp.concatenate([tr, ti], axis=0).astype(BF16))
        y = jnp.dot(f1i_ref[...], jnp.concatenate(rhs, axis=1), preferred_element_type=F32)
        return [y[:, i * r:(i + 1) * r] for i in range(m)]

    def cmul(a, g):
        return a[0] * g[0] - a[1] * g[1], a[0] * g[1] + a[1] * g[0]

    def sconv(ref, b, c, gi):
        z = ref[b, c]
        return (sb_ref[gi] + sw_ref[gi] * shift_prev(z) + sw_ref[HY_IN + gi] * z
                + sw_ref[2 * HY_IN + gi] * shift_next(z))

    nb = HY_CH_BATCH

    def body(t, carry):
        cs = [t * nb + i for i in range(nb)]
        chs = [j * cb + c for c in cs]
        pairs = [(c, ch, b) for c, ch in zip(cs, chs) for b in range(bsz)]
        v = [sconv(v_ref, b, c, ch) for c, ch, b in pairs]
        filt = [h_ref[o, d, c] for c in cs for o in range(HY_ORDER) for d in range(2)]
        spec = fwd(v + filt)
        nv = len(v)
        g = []
        for i in range(nb):
            gi = []
            for o in range(HY_ORDER):
                hf = spec[nv + (i * HY_ORDER + o) * 2]
                hb = spec[nv + (i * HY_ORDER + o) * 2 + 1]
                gi.append((hf[0] + hb[0], hf[1] - hb[1]))
            g.append(gi)
        conv = inv([cmul(spec[n], g[n // bsz][0]) for n in range(nv)])
        y = [sconv(x1_ref, b, c, HY_WIDTH + ch) * (conv[n] + fb_ref[ch] * v[n])
             for n, (c, ch, b) in enumerate(pairs)]
        spec = fwd(y)
        conv = inv([cmul(spec[n], g[n // bsz][1]) for n in range(nv)])
        for n, (c, ch, b) in enumerate(pairs):
            o_ref[b, c] = (sconv(x2_ref, b, c, 2 * HY_WIDTH + ch)
                           * (conv[n] + fb_ref[HY_WIDTH + ch] * y[n]))
        return carry

    lax.fori_loop(0, cb // nb, body, 0)


def _hyena(zhyT, hT, short_w, short_b, filt_bias):
    bsz, _, seq = zhyT.shape
    tabs = _dft_tables(seq)
    r = tabs[1].shape[0]
    hh = r // 2
    z4 = zhyT.reshape(bsz, HY_IN, hh, r)
    h5 = hT.reshape(HY_ORDER, 2, HY_WIDTH, hh, r)
    cb = HY_CH_BLOCK
    nblk = HY_WIDTH // cb
    smem = pl.BlockSpec(memory_space=pltpu.SMEM)
    full = lambda a: pl.BlockSpec(a.shape, lambda j: (0,) * a.ndim)
    zspec = lambda off: pl.BlockSpec((bsz, cb, hh, r), lambda j: (0, off * nblk + j, 0, 0))
    out = pl.pallas_call(
        _hyena_kernel,
        grid=(nblk,),
        in_specs=[smem, smem, smem, zspec(0), zspec(1), zspec(2),
                  pl.BlockSpec((HY_ORDER, 2, cb, hh, r), lambda j: (0, 0, j, 0, 0))]
                 + [full(t) for t in tabs],
        out_specs=pl.BlockSpec((bsz, cb, hh, r), lambda j: (0, j, 0, 0)),
        out_shape=jax.ShapeDtypeStruct((bsz, HY_WIDTH, hh, r), F32),
        scratch_shapes=[pltpu.VMEM(tabs[i].shape, BF16) for i in (0, 3, 4, 5)],
        compiler_params=_params("parallel"),
        name="hyena",
    )(short_w.reshape(-1), short_b, filt_bias.reshape(-1), z4, z4, z4, h5, *tabs)
    return out.reshape(bsz, HY_WIDTH, seq)


def _attn_kernel(bounded_ref, q_ref, k_ref, vt_ref, o_ref):
    g, tq, hd = q_ref.shape[1:]
    seq = k_ref.shape[2]
    rows = vt_ref.shape[2]
    n_chunks = seq // KV_CHUNK
    q = q_ref[0].reshape(g * tq, hd)

    def scores(kc):
        kb = k_ref[0, 0, kc * KV_CHUNK:(kc + 1) * KV_CHUNK, :]
        return lax.dot_general(kb, q, NT_DIMS, preferred_element_type=F32)

    def values(kc):
        return vt_ref[0, 0, :, kc * KV_CHUNK:(kc + 1) * KV_CHUNK]

    def finish(acc):
        out = acc[:hd] / acc[hd:hd + 1]
        o_ref[0] = out.T.reshape(g, tq, hd).astype(o_ref.dtype)

    @pl.when(bounded_ref[0] == 1)
    def _():
        acc = jnp.zeros((rows, g * tq), F32)
        for kc in range(n_chunks):
            p = jnp.exp2(scores(kc)).astype(BF16)
            acc = acc + jnp.dot(values(kc), p, preferred_element_type=F32)
        finish(acc)

    @pl.when(bounded_ref[0] != 1)
    def _():
        m = jnp.full((1, g * tq), -jnp.inf, F32)
        acc = jnp.zeros((rows, g * tq), F32)
        s_next = scores(0)
        for kc in range(n_chunks):
            s = s_next
            if kc + 1 < n_chunks:
                s_next = scores(kc + 1)
            m_new = jnp.maximum(m, jnp.max(s, axis=0, keepdims=True))
            p = jnp.exp2(s - m_new).astype(BF16)
            acc = jnp.exp2(m - m_new) * acc + jnp.dot(values(kc), p, preferred_element_type=F32)
            m = m_new
        finish(acc)


def _attention(q, k, v, q_gain, k_gain):
    bsz, seq, _ = q.shape
    g = N_HEADS // N_KV_HEADS
    qh = q.reshape(bsz, seq, N_HEADS, HEAD_DIM).transpose(0, 2, 1, 3)
    kh = k.reshape(bsz, seq, N_KV_HEADS, HEAD_DIM).transpose(0, 2, 1, 3)
    vt = v.reshape(bsz, seq, N_KV_HEADS, HEAD_DIM).transpose(0, 2, 3, 1)
    vt = jnp.concatenate([vt, jnp.ones((bsz, N_KV_HEADS, ONES_ROWS, seq), BF16)], axis=2)
    rows = HEAD_DIM + ONES_ROWS
    bound = HEAD_DIM * Q_SCALE * BF16_NORM_SLACK * jnp.max(jnp.abs(q_gain)) * jnp.max(jnp.abs(k_gain))
    bounded = (bound <= SAFE_SCORE_BOUND).astype(I32).reshape(1)
    tq = Q_TILE
    grid_spec = pltpu.PrefetchScalarGridSpec(
        num_scalar_prefetch=1,
        grid=(bsz, N_KV_HEADS, seq // tq),
        in_specs=[pl.BlockSpec((1, g, tq, HEAD_DIM), lambda b, h, i, f: (b, h, i, 0)),
                  pl.BlockSpec((1, 1, seq, HEAD_DIM), lambda b, h, i, f: (b, h, 0, 0)),
                  pl.BlockSpec((1, 1, rows, seq), lambda b, h, i, f: (b, h, 0, 0))],
        out_specs=pl.BlockSpec((1, g, tq, HEAD_DIM), lambda b, h, i, f: (b, h, i, 0)))
    o = pl.pallas_call(
        _attn_kernel,
        grid_spec=grid_spec,
        out_shape=jax.ShapeDtypeStruct((bsz, N_HEADS, seq, HEAD_DIM), BF16),
        compiler_params=_params("parallel", "parallel", "parallel"),
        name="attention",
    )(bounded, qh, kh, vt)
    return o.transpose(0, 2, 1, 3).reshape(bsz, seq, ATT_WIDTH)


def _merge_kernel(x_ref, yhyT_ref, yat_ref, gate_ref, mod_ref, g_ref, why_ref, wat_ref, wout_ref, wrT_ref,
                  x1_ref, h2_ref, affT_ref):
    d = x_ref.shape[2]
    yhy = yhyT_ref[0].T.astype(BF16)
    a = jnp.dot(yhy, why_ref[...], preferred_element_type=F32)
    b = jnp.dot(yat_ref[0], wat_ref[...], preferred_element_type=F32)
    gate = gate_ref[0]
    mrg = gate[:, :d] * a + gate[:, d:] * b
    o = jnp.dot(mrg.astype(BF16), wout_ref[...], preferred_element_type=F32)
    gt1 = mod_ref[0, 2:3, :]
    sh2 = mod_ref[0, 3:4, :]
    sc2 = mod_ref[0, 4:5, :]
    x1 = x_ref[0] + gt1 * o
    x1_ref[0] = x1
    ms = jnp.mean(x1 * x1, axis=-1, keepdims=True)
    h2 = (x1 * lax.rsqrt(ms + EPS)) * g_ref[...] * (1.0 + sc2) + sh2
    h2_ref[0] = h2.astype(BF16)
    logT = lax.dot_general(wrT_ref[...], h2, NT_DIMS, precision=HIGHEST, preferred_element_type=F32)
    mx = jnp.max(logT, axis=0, keepdims=True)
    ex = jnp.exp(logT - mx)
    affT_ref[0] = ex / jnp.sum(ex, axis=0, keepdims=True)


def _merge(x, yhyT, yat, gates, mod3, g_ffn, w_hy_out, w_att_out, w_out, w_router):
    bsz, seq, d = x.shape
    tm = TOKEN_TILE
    ne = w_router.shape[1]
    full = lambda shape: pl.BlockSpec(shape, lambda b, i: (0,) * len(shape))
    return pl.pallas_call(
        _merge_kernel,
        grid=(bsz, seq // tm),
        in_specs=[pl.BlockSpec((1, tm, d), lambda b, i: (b, i, 0)),
                  pl.BlockSpec((1, HY_WIDTH, tm), lambda b, i: (b, 0, i)),
                  pl.BlockSpec((1, tm, ATT_WIDTH), lambda b, i: (b, i, 0)),
                  pl.BlockSpec((1, tm, 2 * d), lambda b, i: (b, i, 0)),
                  pl.BlockSpec((1, 6, d), lambda b, i: (b, 0, 0)),
                  full((1, d)), full((HY_WIDTH, d)), full((ATT_WIDTH, d)), full((d, d)), full((ne, d))],
        out_specs=[pl.BlockSpec((1, tm, d), lambda b, i: (b, i, 0)),
                   pl.BlockSpec((1, tm, d), lambda b, i: (b, i, 0)),
                   pl.BlockSpec((1, ne, tm), lambda b, i: (b, 0, i))],
        out_shape=[jax.ShapeDtypeStruct((bsz, seq, d), F32),
                   jax.ShapeDtypeStruct((bsz, seq, d), BF16),
                   jax.ShapeDtypeStruct((bsz, ne, seq), F32)],
        compiler_params=_params("parallel", "parallel"),
        name="merge",
    )(x, yhyT, yat, gates, mod3, g_ffn.reshape(1, d), w_hy_out.astype(BF16), w_att_out.astype(BF16),
      w_out.astype(BF16), w_router.T)


def _topk_kernel(a_ref, ut_ref, pos_ref, *, cap):
    nr, seq = a_ref.shape
    a = a_ref[...]

    def count(mask):
        return jnp.sum(mask.astype(F32), axis=1, keepdims=True)

    def probe(t, lo, hi):
        ok = count(a >= t) >= cap
        return ok, jnp.where(ok, t, lo), jnp.where(ok, hi, t)

    def by_exponent(_, c):
        elo, ehi, lo, hi = c
        emid = 0.5 * (elo + ehi)
        ok, lo, hi = probe(jnp.exp2(emid), lo, hi)
        return jnp.where(ok, emid, elo), jnp.where(ok, ehi, emid), lo, hi

    def by_value(_, c):
        lo, hi = c
        _, lo, hi = probe(0.5 * (lo + hi), lo, hi)
        return lo, hi

    col = lambda v: jnp.full((nr, 1), v, F32)
    _, _, lo, hi = lax.fori_loop(0, TOPK_EXP_STEPS, by_exponent, (col(F32_MIN_EXP), col(1.0), col(0.0), col(2.0)))
    lo, hi = lax.fori_loop(0, TOPK_VALUE_STEPS, by_value, (lo, hi))
    gt = a >= hi
    eq = (a >= lo) & (a < hi)
    need = cap - count(gt)
    ut = ut_ref[...]
    run_eq = jnp.zeros((nr, 1), F32)
    run_sel = jnp.zeros((nr, 1), F32)
    for ch in range(seq // LANES):
        sl = slice(ch * LANES, (ch + 1) * LANES)
        eq_rank = jnp.dot(eq[:, sl].astype(BF16), ut, preferred_element_type=F32) + run_eq
        run_eq = eq_rank[:, LANES - 1:LANES]
        sel = gt[:, sl] | (eq[:, sl] & (eq_rank <= need))
        pos = jnp.dot(sel.astype(BF16), ut, preferred_element_type=F32) + run_sel
        run_sel = pos[:, LANES - 1:LANES]
        pos_ref[:, sl] = jnp.where(sel, pos.astype(I32) - 1, -1)


def _topk(aff2, cap):
    nr, seq = aff2.shape
    ut = jnp.asarray(np.triu(np.ones((LANES, LANES), np.float32))).astype(BF16)
    return pl.pallas_call(
        functools.partial(_topk_kernel, cap=cap),
        grid=(1,),
        in_specs=[pl.BlockSpec((nr, seq), lambda i: (0, 0)), pl.BlockSpec((LANES, LANES), lambda i: (0, 0))],
        out_specs=pl.BlockSpec((nr, seq), lambda i: (0, 0)),
        out_shape=jax.ShapeDtypeStruct((nr, seq), I32),
        compiler_params=_params("arbitrary"),
        name="topk",
    )(aff2, ut)


def _onehot(pos_row, r, tile):
    dest = lax.broadcasted_iota(I32, (tile, tile), 0) + r * tile
    return (pos_row == dest).astype(BF16)


def _moe_kernel(lo_ref, hi_ref, pos_ref, a_ref, h2_ref, wg_ref, wu_ref, wd_ref, ye_ref, xe_ref, w_ref, *, nblk):
    tile = TOKEN_TILE
    ne = pl.num_programs(1)
    base = (pl.program_id(0) * ne + pl.program_id(1)) * nblk
    for r in range(nblk):
        xe_ref[...] = jnp.zeros_like(xe_ref)
        w_ref[...] = jnp.zeros_like(w_ref)

        def body(i, carry):
            st = pl.multiple_of(i * tile, tile)
            p = _onehot(pos_ref[0, 0, :, pl.ds(st, tile)], r, tile)
            xe_ref[...] += jnp.dot(p, h2_ref[0, pl.ds(st, tile), :], preferred_element_type=F32)
            w_ref[...] += lax.dot_general(p, a_ref[0, 0, :, pl.ds(st, tile)], NT_DIMS,
                                          preferred_element_type=F32)
            return carry

        lax.fori_loop(lo_ref[base + r], hi_ref[base + r], body, 0)
        xe = xe_ref[...].astype(BF16)
        a = jnp.dot(xe, wg_ref[0], preferred_element_type=F32)
        u = jnp.dot(xe, wu_ref[0], preferred_element_type=F32)
        hmid = (a * jax.nn.sigmoid(a) * u).astype(BF16)
        ye = jnp.dot(hmid, wd_ref[0], preferred_element_type=F32)
        w = w_ref[:, 0:1] + w_ref[:, 1:2] + w_ref[:, 2:3]
        ye_ref[0, 0, r * tile:(r + 1) * tile, :] = (ye * w).astype(BF16)


def _moe(tile_lo, tile_hi, pos4, a4, h2, wg, wu, wd, cap):
    bsz, seq, d = h2.shape
    ne, _, dff = wg.shape
    nblk = cap // TOKEN_TILE
    grid_spec = pltpu.PrefetchScalarGridSpec(
        num_scalar_prefetch=2,
        grid=(bsz, ne),
        in_specs=[pl.BlockSpec((1, 1, 1, seq), lambda b, e, lo, hi: (b, e, 0, 0)),
                  pl.BlockSpec((1, 1, A_ROWS, seq), lambda b, e, lo, hi: (b, e, 0, 0)),
                  pl.BlockSpec((1, seq, d), lambda b, e, lo, hi: (b, 0, 0), pipeline_mode=pl.Buffered(1)),
                  pl.BlockSpec((1, d, dff), lambda b, e, lo, hi: (e, 0, 0)),
                  pl.BlockSpec((1, d, dff), lambda b, e, lo, hi: (e, 0, 0)),
                  pl.BlockSpec((1, dff, d), lambda b, e, lo, hi: (e, 0, 0))],
        out_specs=pl.BlockSpec((1, 1, cap, d), lambda b, e, lo, hi: (b, e, 0, 0)),
        scratch_shapes=[pltpu.VMEM((TOKEN_TILE, d), F32), pltpu.VMEM((TOKEN_TILE, A_ROWS), F32)])
    return pl.pallas_call(
        functools.partial(_moe_kernel, nblk=nblk),
        grid_spec=grid_spec,
        out_shape=jax.ShapeDtypeStruct((bsz, ne, cap, d), BF16),
        compiler_params=_params("parallel", "arbitrary"),
        name="moe",
    )(tile_lo, tile_hi, pos4, a4, h2, wg, wu, wd)


def _combine_kernel(lo_ref, hi_ref, pos_ref, ye_ref, x1_ref, mod_ref, o_ref, *, nblk):
    tile = TOKEN_TILE
    tiles_here = o_ref.shape[1] // tile
    ne = pl.num_programs(2)
    e = pl.program_id(2)
    first = pl.program_id(1) * tiles_here
    base = (pl.program_id(0) * ne + e) * nblk

    @pl.when(e == 0)
    def _():
        o_ref[...] = jnp.zeros_like(o_ref)

    for r in range(nblk):
        yer = ye_ref[0, 0, r * tile:(r + 1) * tile, :]

        def body(i, carry):
            src = pl.multiple_of(i * tile, tile)
            dst = pl.multiple_of((i - first) * tile, tile)
            p = _onehot(pos_ref[0, 0, :, pl.ds(src, tile)], r, tile)
            o_ref[0, pl.ds(dst, tile), :] += lax.dot_general(p, yer, TN_DIMS, preferred_element_type=F32)
            return carry

        lo = jnp.maximum(lo_ref[base + r], first)
        hi = jnp.minimum(hi_ref[base + r], first + tiles_here)
        lax.fori_loop(lo, hi, body, 0)

    @pl.when(e == ne - 1)
    def _():
        o_ref[0] = x1_ref[0] + mod_ref[0, 5:6, :] * o_ref[0]


def _combine(tile_lo, tile_hi, pos4, ye, x1, mod3, cap):
    bsz, seq, d = x1.shape
    ne = ye.shape[1]
    nblk = cap // TOKEN_TILE
    tb = seq // COMBINE_SPLIT
    grid_spec = pltpu.PrefetchScalarGridSpec(
        num_scalar_prefetch=2,
        grid=(bsz, COMBINE_SPLIT, ne),
        in_specs=[pl.BlockSpec((1, 1, 1, seq), lambda b, t, e, lo, hi: (b, e, 0, 0)),
                  pl.BlockSpec((1, 1, cap, d), lambda b, t, e, lo, hi: (b, e, 0, 0)),
                  pl.BlockSpec((1, tb, d), lambda b, t, e, lo, hi: (b, t, 0)),
                  pl.BlockSpec((1, 6, d), lambda b, t, e, lo, hi: (b, 0, 0))],
        out_specs=pl.BlockSpec((1, tb, d), lambda b, t, e, lo, hi: (b, t, 0)))
    return pl.pallas_call(
        functools.partial(_combine_kernel, nblk=nblk),
        grid_spec=grid_spec,
        out_shape=jax.ShapeDtypeStruct((bsz, seq, d), F32),
        compiler_params=_params("parallel", "parallel", "arbitrary"),
        name="combine",
    )(tile_lo, tile_hi, pos4, ye, x1, mod3)


def _route_tables(pos, cap):
    bsz, ne, seq = pos.shape
    tile = TOKEN_TILE
    counts = jnp.sum((pos >= 0).reshape(bsz, ne, seq // tile, tile), axis=-1, dtype=I32)
    ends = jnp.cumsum(counts, axis=-1)
    starts = ends - counts
    edges = jnp.arange(cap // tile, dtype=I32) * tile
    lo = jnp.sum(ends[..., None, :] <= edges[:, None], axis=-1, dtype=I32)
    hi = jnp.sum(starts[..., None, :] < (edges[:, None] + tile), axis=-1, dtype=I32)
    return lo.reshape(-1), hi.reshape(-1)


def _split_bf16(a, rows):
    hi = a.astype(BF16)
    r1 = a - hi.astype(F32)
    mid = r1.astype(BF16)
    lo = (r1 - mid.astype(F32)).astype(BF16)
    pad = jnp.zeros(a.shape[:-1] + (rows - 3, a.shape[-1]), BF16)
    return jnp.concatenate([hi[..., None, :], mid[..., None, :], lo[..., None, :], pad], axis=-2)


def kernel(x, c, w_ada, b_ada, g_mix, g_ffn, w_in, b_in, short_w, short_b, hy_w1, hy_b1, hy_w2, hy_b2, hy_w3, hy_b3, hy_w4, hy_freq, hy_bias, q_gain, k_gain, w_hy_out, w_att_out, w_out, w_router, w_gate, w_up, w_down):
    bsz, seq, d = x.shape
    depth = w_ada.shape[0]
    ne = w_router.shape[-1]
    cap = EC_FACTOR * seq // ne
    for l in range(depth):
        mod3 = _adaln(c, w_ada[l], b_ada[l]).reshape(bsz, 6, d)
        zhyT, q, k, v, gates = _inproj(x, mod3, g_mix[l], w_in[l], b_in[l], q_gain[l], k_gain[l])
        hT = _filters(seq, hy_w1[l], hy_b1[l], hy_w2[l], hy_b2[l], hy_w3[l], hy_b3[l], hy_w4[l], hy_freq[l])
        yhyT = _hyena(zhyT, hT, short_w[l], short_b[l], hy_bias[l])
        yat = _attention(q, k, v, q_gain[l], k_gain[l])
        x1, h2, affT = _merge(x, yhyT, yat, gates, mod3, g_ffn[l], w_hy_out[l], w_att_out[l], w_out[l],
                              w_router[l])
        pos = _topk(affT.reshape(bsz * ne, seq), cap).reshape(bsz, ne, seq)
        tile_lo, tile_hi = _route_tables(pos, cap)
        pos4 = pos.reshape(bsz, ne, 1, seq)
        ye = _moe(tile_lo, tile_hi, pos4, _split_bf16(affT, A_ROWS), h2,
                  w_gate[l].astype(BF16), w_up[l].astype(BF16), w_down[l].astype(BF16), cap)
        x = _combine(tile_lo, tile_hi, pos4, ye, x1, mod3, cap)
    return x
```

```python
import functools
import math

import numpy as np
import jax
import jax.numpy as jnp
from jax import lax
from jax.experimental import pallas as pl
from jax.experimental.pallas import tpu as pltpu

F32 = jnp.float32
BF16 = jnp.bfloat16
I32 = jnp.int32
HIGHEST = lax.Precision.HIGHEST

EPS = 1e-6
GRID_W = 64
HY_WIDTH = 512
HY_ORDER = 2
SHORT_K = 3
FILTER_EMB = 33
FILTER_HIDDEN = 64
HY_QUICK_DECAY_PCT = 0.3
HY_GRADUAL_DECAY_PCT = 1.5
HY_DECAY_TARGET = 1e-2
N_HEADS = 8
N_KV_HEADS = 2
HEAD_DIM = 64
ROPE_THETA = 10000.0
N_EXPERTS = 16
EC_FACTOR = 2
HY_IN = 3 * HY_WIDTH
ATT_WIDTH = N_HEADS * HEAD_DIM
KV_WIDTH = N_KV_HEADS * HEAD_DIM

LANES = 128
VMEM_LIMIT = 56 * 1024 * 1024
MXU_TILE = 256
TOKEN_TILE = 256
Q_TILE = 128
KV_CHUNK = 1024
HY_CH_BLOCK = 8
HY_CH_BATCH = 4
FILTER_ROWS = 128
WINDOW = 64
BF16_SUBLANES = 16
GATHER_SPLIT = 2
SAFE_SCORE_BOUND = 48.0
BF16_NORM_SLACK = 1.01
F32_MIN_EXP = -150.0
TOPK_EXP_STEPS = 10
TOPK_VALUE_STEPS = 28
ONES_ROWS = 16
Q_SCALE = HEAD_DIM ** -0.5 * math.log2(math.e)

NT_DIMS = (((1,), (1,)), ((), ()))


def _params(*sem):
    return pltpu.CompilerParams(dimension_semantics=sem, vmem_limit_bytes=VMEM_LIMIT)


def _adaln_kernel(c_ref, w_ref, b_ref, o_ref):
    c = c_ref[...]
    s = c * jax.nn.sigmoid(c)
    o_ref[...] = jnp.dot(s, w_ref[...], precision=HIGHEST, preferred_element_type=F32) + b_ref[...]


def _adaln(c, w, b):
    bsz, d = c.shape
    n = w.shape[1]
    rows = 8
    cp = jnp.zeros((rows, d), F32).at[:bsz].set(c)
    tn = 1536
    out = pl.pallas_call(
        _adaln_kernel,
        grid=(n // tn,),
        in_specs=[pl.BlockSpec((rows, d), lambda j: (0, 0)),
                  pl.BlockSpec((d, tn), lambda j: (0, j)),
                  pl.BlockSpec((1, tn), lambda j: (0, j))],
        out_specs=pl.BlockSpec((rows, tn), lambda j: (0, j)),
        out_shape=jax.ShapeDtypeStruct((rows, n), F32),
        compiler_params=_params("parallel"),
        name="adaln",
    )(cp, w, b.reshape(1, n))
    return out[:bsz]


def _inproj_kernel(x_ref, mod_ref, g_ref, whyT_ref, bhy_ref, wr_ref, br_ref, cos_ref, sin_ref,
                   qg_ref, kg_ref, bd_ref, zhyT_ref, q_ref, k_ref, v_ref, gate_ref):
    x = x_ref[0]
    ms = jnp.mean(x * x, axis=-1, keepdims=True)
    sh1 = mod_ref[0, 0:1, :]
    sc1 = mod_ref[0, 1:2, :]
    h = (x * lax.rsqrt(ms + EPS)) * g_ref[...] * (1.0 + sc1) + sh1
    hb = h.astype(BF16)
    zhyT_ref[0] = lax.dot_general(whyT_ref[...], hb, NT_DIMS, preferred_element_type=F32) + bhy_ref[...]
    zr = jnp.dot(hb, wr_ref[...], preferred_element_type=F32) + br_ref[...]

    cos = cos_ref[...]
    sin = sin_ref[...]
    bd = bd_ref[...]
    lane = lax.broadcasted_iota(I32, cos.shape, 1)
    first = (lane & (HEAD_DIM // 2 - 1)) < HEAD_DIM // 4

    def norm_rope(u, gain, scale):
        ss = jnp.dot(u * u, bd, precision=HIGHEST, preferred_element_type=F32)
        un = (u * lax.rsqrt(ss * (1.0 / HEAD_DIM) + EPS)) * gain
        rot = jnp.where(first, pltpu.roll(un, LANES - HEAD_DIM // 4, 1), pltpu.roll(un, HEAD_DIM // 4, 1))
        return (un * cos + rot * sin) * scale

    for j in range(ATT_WIDTH // LANES):
        u = zr[:, j * LANES:(j + 1) * LANES]
        q_ref[0, :, j * LANES:(j + 1) * LANES] = norm_rope(u, qg_ref[...], Q_SCALE).astype(BF16)
    k_ref[0] = norm_rope(zr[:, ATT_WIDTH:ATT_WIDTH + KV_WIDTH], kg_ref[...], 1.0).astype(BF16)
    v_ref[0] = zr[:, ATT_WIDTH + KV_WIDTH:ATT_WIDTH + 2 * KV_WIDTH].astype(BF16)
    gate_ref[0] = jax.nn.sigmoid(zr[:, ATT_WIDTH + 2 * KV_WIDTH:])


def _rope_tables(seq):
    rows = seq // GRID_W
    t = np.arange(seq)
    row = (t // GRID_W).astype(np.float32)
    col = (t % GRID_W).astype(np.float32)
    half = HEAD_DIM // 2
    quarter = half // 2
    inv = (ROPE_THETA ** (-np.arange(0, half, 2, dtype=np.float32) / half)).astype(np.float32)
    ang_r = row[:, None] * inv[None, :]
    ang_c = col[:, None] * inv[None, :]
    ang = np.concatenate([ang_r, ang_r, ang_c, ang_c], axis=1)
    sign = np.concatenate([-np.ones(quarter), np.ones(quarter)] * 2).astype(np.float32)
    cos = np.cos(ang).astype(np.float32)
    sin = (np.sin(ang) * sign[None, :]).astype(np.float32)
    reps = LANES // HEAD_DIM
    del rows
    return jnp.asarray(np.tile(cos, (1, reps))), jnp.asarray(np.tile(sin, (1, reps)))


def _inproj(x, mod3, g_mix, w_in, b_in, q_gain, k_gain):
    bsz, seq, d = x.shape
    tm = TOKEN_TILE
    n_rest = w_in.shape[1] - HY_IN
    n_gate = n_rest - ATT_WIDTH - 2 * KV_WIDTH
    whyT = w_in[:, :HY_IN].T.astype(BF16)
    wr = w_in[:, HY_IN:].astype(BF16)
    bhy = b_in[:HY_IN].reshape(HY_IN, 1)
    br = b_in[HY_IN:].reshape(1, n_rest)
    cos, sin = _rope_tables(seq)
    reps = LANES // HEAD_DIM
    qg = jnp.tile(q_gain, reps).reshape(1, LANES)
    kg = jnp.tile(k_gain, reps).reshape(1, LANES)
    bd = jnp.asarray(np.kron(np.eye(reps, dtype=np.float32), np.ones((HEAD_DIM, HEAD_DIM), np.float32)))
    full = lambda shape: pl.BlockSpec(shape, lambda b, i: (0,) * len(shape))
    return pl.pallas_call(
        _inproj_kernel,
        grid=(bsz, seq // tm),
        in_specs=[pl.BlockSpec((1, tm, d), lambda b, i: (b, i, 0)),
                  pl.BlockSpec((1, 6, d), lambda b, i: (b, 0, 0)),
                  full((1, d)), full((HY_IN, d)), full((HY_IN, 1)), full((d, n_rest)), full((1, n_rest)),
                  pl.BlockSpec((tm, LANES), lambda b, i: (i, 0)),
                  pl.BlockSpec((tm, LANES), lambda b, i: (i, 0)),
                  full((1, LANES)), full((1, LANES)), full((LANES, LANES))],
        out_specs=[pl.BlockSpec((1, HY_IN, tm), lambda b, i: (b, 0, i)),
                   pl.BlockSpec((1, tm, ATT_WIDTH), lambda b, i: (b, i, 0)),
                   pl.BlockSpec((1, tm, KV_WIDTH), lambda b, i: (b, i, 0)),
                   pl.BlockSpec((1, tm, KV_WIDTH), lambda b, i: (b, i, 0)),
                   pl.BlockSpec((1, tm, n_gate), lambda b, i: (b, i, 0))],
        out_shape=[jax.ShapeDtypeStruct((bsz, HY_IN, seq), F32),
                   jax.ShapeDtypeStruct((bsz, seq, ATT_WIDTH), BF16),
                   jax.ShapeDtypeStruct((bsz, seq, KV_WIDTH), BF16),
                   jax.ShapeDtypeStruct((bsz, seq, KV_WIDTH), BF16),
                   jax.ShapeDtypeStruct((bsz, seq, n_gate), F32)],
        compiler_params=_params("parallel", "parallel"),
        name="inproj",
    )(x, mod3, g_mix.reshape(1, d), whyT, bhy, wr, br, cos, sin, qg, kg, bd)


def _filter_kernel(zT_ref, w1T_ref, b1_ref, w2T_ref, b2_ref, w3T_ref, b3_ref, fr_ref, w4T_ref,
                   t01_ref, absd_ref, o_ref, hdn_ref):
    @pl.when(pl.program_id(0) == 0)
    def _():
        fr = fr_ref[...]
        h = jnp.sin(fr[:, 0:1] * (jnp.dot(w1T_ref[...], zT_ref[...], precision=HIGHEST,
                                          preferred_element_type=F32) + b1_ref[...]))
        h = jnp.sin(fr[:, 1:2] * (jnp.dot(w2T_ref[...], h, precision=HIGHEST,
                                          preferred_element_type=F32) + b2_ref[...]))
        h = jnp.sin(fr[:, 2:3] * (jnp.dot(w3T_ref[...], h, precision=HIGHEST,
                                          preferred_element_type=F32) + b3_ref[...]))
        hdn_ref[...] = h

    hT = jnp.dot(w4T_ref[...], hdn_ref[...], precision=HIGHEST, preferred_element_type=F32)
    hT = hT * jnp.exp(-t01_ref[...] * absd_ref[...])
    nrm = jnp.sum(jnp.abs(hT), axis=1, keepdims=True) + EPS
    o_ref[...] = hT / nrm


def _filters(seq, w1, b1, w2, b2, w3, b3, w4, freq):
    emb_pad = 64
    bands = (FILTER_EMB - 1) // 2
    t01 = np.linspace(0.0, 1.0, seq, dtype=np.float32)[None, :]
    f = np.linspace(1e-4, bands - 1, bands, dtype=np.float32)[:, None]
    w = ((2.0 * math.pi) * np.arange(seq, dtype=np.float32) / seq).astype(np.float32)[None, :]
    zT = np.zeros((emb_pad, seq), np.float32)
    zT[0:1] = t01
    zT[1:1 + bands] = np.cos(f * w)
    zT[1 + bands:1 + 2 * bands] = -np.sin(f * w)
    w1T = jnp.zeros((FILTER_HIDDEN, emb_pad), F32).at[:, :FILTER_EMB].set(w1.T)
    max_decay = math.log(HY_DECAY_TARGET) / HY_QUICK_DECAY_PCT
    min_decay = math.log(HY_DECAY_TARGET) / HY_GRADUAL_DECAY_PCT
    deltas = np.abs(np.linspace(min_decay, max_decay, HY_WIDTH, dtype=np.float32))
    n_rows = HY_ORDER * 2 * HY_WIDTH
    absd = np.tile(deltas, HY_ORDER * 2).reshape(n_rows, 1)
    rb = FILTER_ROWS
    full = lambda shape: pl.BlockSpec(shape, lambda j: (0,) * len(shape))
    col = lambda v: v.reshape(FILTER_HIDDEN, 1)
    return pl.pallas_call(
        _filter_kernel,
        grid=(n_rows // rb,),
        in_specs=[full((emb_pad, seq)),
                  full((FILTER_HIDDEN, emb_pad)), full((FILTER_HIDDEN, 1)),
                  full((FILTER_HIDDEN, FILTER_HIDDEN)), full((FILTER_HIDDEN, 1)),
                  full((FILTER_HIDDEN, FILTER_HIDDEN)), full((FILTER_HIDDEN, 1)),
                  full((FILTER_HIDDEN, 3)),
                  pl.BlockSpec((rb, FILTER_HIDDEN), lambda j: (j, 0)),
                  full((1, seq)),
                  pl.BlockSpec((rb, 1), lambda j: (j, 0))],
        out_specs=pl.BlockSpec((rb, seq), lambda j: (j, 0)),
        out_shape=jax.ShapeDtypeStruct((n_rows, seq), F32),
        scratch_shapes=[pltpu.VMEM((FILTER_HIDDEN, seq), F32)],
        compiler_params=_params("arbitrary"),
        name="filters",
    )(jnp.asarray(zT), w1T, col(b1), w2.T, col(b2), w3.T, col(b3), freq.T, w4.T,
      jnp.asarray(t01), jnp.asarray(absd))


def _dft_tables(seq):
    n = 2 * seq
    r = int(round(math.sqrt(n)))
    assert r * r == n, "2*seq must be a perfect square"
    hh = r // 2
    k = np.arange(r, dtype=np.float64)
    ang = 2.0 * np.pi * np.outer(k, k) / r
    fr, fi = np.cos(ang), -np.sin(ang)
    angt = 2.0 * np.pi * np.outer(k, k) / n
    twr, twi = np.cos(angt), -np.sin(angt)
    f1 = np.concatenate([fr[:, :hh], fi[:, :hh]], axis=0)
    f3 = np.block([[fr, fi], [-fi, fr]])
    f3i = np.block([[fr, -fi], [fi, fr]])
    f1i = np.concatenate([fr[:hh, :], fi[:hh, :]], axis=1) / n
    as32 = lambda a: jnp.asarray(a.astype(np.float32))
    return as32(f1), as32(twr), as32(twi), as32(f3), as32(f3i), as32(f1i)


def _hyena_kernel(sw_ref, sb_ref, fb_ref, v_ref, x1_ref, x2_ref, h_ref, f1_f32, twr_ref, twi_ref,
                  f3_f32, f3i_f32, f1i_f32, o_ref, f1_ref, f3_ref, f3i_ref, f1i_ref):
    bsz, cb, hh, r = v_ref.shape
    j = pl.program_id(0)
    f1_ref[...] = f1_f32[...].astype(BF16)
    f3_ref[...] = f3_f32[...].astype(BF16)
    f3i_ref[...] = f3i_f32[...].astype(BF16)
    f1i_ref[...] = f1i_f32[...].astype(BF16)
    twr = twr_ref[...]
    twi = twi_ref[...]
    lane = lax.broadcasted_iota(I32, (hh, r), 1)
    row = lax.broadcasted_iota(I32, (hh, r), 0)

    def shift_prev(z):
        a = pltpu.roll(z, 1, 1)
        a = jnp.where(lane == 0, pltpu.roll(a, 1, 0), a)
        return jnp.where((lane == 0) & (row == 0), 0.0, a)

    def shift_next(z):
        a = pltpu.roll(z, r - 1, 1)
        a = jnp.where(lane == r - 1, pltpu.roll(a, hh - 1, 0), a)
        return jnp.where((lane == r - 1) & (row == hh - 1), 0.0, a)

    def fwd(slabs):
        m = len(slabs)
        acat = jnp.concatenate([a.astype(BF16) for a in slabs], axis=1)
        s = jnp.dot(f1_ref[...], acat, preferred_element_type=F32)
        lhs = []
        for i in range(m):
            br = s[:r, i * r:(i + 1) * r]
            bi = s[r:, i * r:(i + 1) * r]
            cr = br * twr - bi * twi
            ci = br * twi + bi * twr
            lhs.append(jnp.concatenate([cr, ci], axis=1).astype(BF16))
        d = jnp.dot(jnp.concatenate(lhs, axis=0), f3_ref[...], preferred_element_type=F32)
        return [(d[i * r:(i + 1) * r, :r], d[i * r:(i + 1) * r, r:]) for i in range(m)]

    def inv(specs):
        m = len(specs)
        lhs = jnp.concatenate([jnp.concatenate([pr, pi], axis=1).astype(BF16) for pr, pi in specs], axis=0)
        e = jnp.dot(lhs, f3i_ref[...], preferred_element_type=F32)
        rhs = []
        for i in range(m):
            er = e[i * r:(i + 1) * r, :r]
            ei = e[i * r:(i + 1) * r, r:]
            tr = er * twr + ei * twi
            ti = ei * twr - er * twi
            rhs.append(jnp.concatenate([tr, ti], axis=0).astype(BF16))
        y = jnp.dot(f1i_ref[...], jnp.concatenate(rhs, axis=1), preferred_element_type=F32)
        return [y[:, i * r:(i + 1) * r] for i in range(m)]

    def cmul(a, g):
        return a[0] * g[0] - a[1] * g[1], a[0] * g[1] + a[1] * g[0]

    def sconv(ref, b, c, gi):
        z = ref[b, c]
        return (sb_ref[gi] + sw_ref[gi] * shift_prev(z) + sw_ref[HY_IN + gi] * z
                + sw_ref[2 * HY_IN + gi] * shift_next(z))

    nb = HY_CH_BATCH

    def body(t, carry):
        cs = [t * nb + i for i in range(nb)]
        chs = [j * cb + c for c in cs]
        pairs = [(c, ch, b) for c, ch in zip(cs, chs) for b in range(bsz)]
        v = [sconv(v_ref, b, c, ch) for c, ch, b in pairs]
        filt = [h_ref[o, d, c] for c in cs for o in range(HY_ORDER) for d in range(2)]
        spec = fwd(v + filt)
        nv = len(v)
        g = []
        for i in range(nb):
            gi = []
            for o in range(HY_ORDER):
                hf = spec[nv + (i * HY_ORDER + o) * 2]
                hb = spec[nv + (i * HY_ORDER + o) * 2 + 1]
                gi.append((hf[0] + hb[0], hf[1] - hb[1]))
            g.append(gi)
        conv = inv([cmul(spec[n], g[n // bsz][0]) for n in range(nv)])
        y = [sconv(x1_ref, b, c, HY_WIDTH + ch) * (conv[n] + fb_ref[ch] * v[n])
             for n, (c, ch, b) in enumerate(pairs)]
        spec = fwd(y)
        conv = inv([cmul(spec[n], g[n // bsz][1]) for n in range(nv)])
        for n, (c, ch, b) in enumerate(pairs):
            o_ref[b, c] = (sconv(x2_ref, b, c, 2 * HY_WIDTH + ch)
                           * (conv[n] + fb_ref[HY_WIDTH + ch] * y[n]))
        return carry

    lax.fori_loop(0, cb // nb, body, 0)


def _hyena(zhyT, hT, short_w, short_b, filt_bias):
    bsz, _, seq = zhyT.shape
    tabs = _dft_tables(seq)
    r = tabs[1].shape[0]
    hh = r // 2
    z4 = zhyT.reshape(bsz, HY_IN, hh, r)
    h5 = hT.reshape(HY_ORDER, 2, HY_WIDTH, hh, r)
    cb = HY_CH_BLOCK
    nblk = HY_WIDTH // cb
    smem = pl.BlockSpec(memory_space=pltpu.SMEM)
    full = lambda a: pl.BlockSpec(a.shape, lambda j: (0,) * a.ndim)
    zspec = lambda off: pl.BlockSpec((bsz, cb, hh, r), lambda j: (0, off * nblk + j, 0, 0))
    out = pl.pallas_call(
        _hyena_kernel,
        grid=(nblk,),
        in_specs=[smem, smem, smem, zspec(0), zspec(1), zspec(2),
                  pl.BlockSpec((HY_ORDER, 2, cb, hh, r), lambda j: (0, 0, j, 0, 0))]
                 + [full(t) for t in tabs],
        out_specs=pl.BlockSpec((bsz, cb, hh, r), lambda j: (0, j, 0, 0)),
        out_shape=jax.ShapeDtypeStruct((bsz, HY_WIDTH, hh, r), F32),
        scratch_shapes=[pltpu.VMEM(tabs[i].shape, BF16) for i in (0, 3, 4, 5)],
        compiler_params=_params("parallel"),
        name="hyena",
    )(short_w.reshape(-1), short_b, filt_bias.reshape(-1), z4, z4, z4, h5, *tabs)
    return out.reshape(bsz, HY_WIDTH, seq)


def _attn_kernel(bounded_ref, q_ref, k_ref, vt_ref, o_ref):
    g, tq, hd = q_ref.shape[1:]
    seq = k_ref.shape[2]
    rows = vt_ref.shape[2]
    n_chunks = seq // KV_CHUNK
    q = q_ref[0].reshape(g * tq, hd)

    def scores(kc):
        kb = k_ref[0, 0, kc * KV_CHUNK:(kc + 1) * KV_CHUNK, :]
        return lax.dot_general(kb, q, NT_DIMS, preferred_element_type=F32)

    def values(kc):
        return vt_ref[0, 0, :, kc * KV_CHUNK:(kc + 1) * KV_CHUNK]

    def finish(acc):
        out = acc[:hd] / acc[hd:hd + 1]
        o_ref[0] = out.T.reshape(g, tq, hd).astype(o_ref.dtype)

    @pl.when(bounded_ref[0] == 1)
    def _():
        acc = jnp.zeros((rows, g * tq), F32)
        for kc in range(n_chunks):
            p = jnp.exp2(scores(kc)).astype(BF16)
            acc = acc + jnp.dot(values(kc), p, preferred_element_type=F32)
        finish(acc)

    @pl.when(bounded_ref[0] != 1)
    def _():
        m = jnp.full((1, g * tq), -jnp.inf, F32)
        acc = jnp.zeros((rows, g * tq), F32)
        s_next = scores(0)
        for kc in range(n_chunks):
            s = s_next
            if kc + 1 < n_chunks:
                s_next = scores(kc + 1)
            m_new = jnp.maximum(m, jnp.max(s, axis=0, keepdims=True))
            p = jnp.exp2(s - m_new).astype(BF16)
            acc = jnp.exp2(m - m_new) * acc + jnp.dot(values(kc), p, preferred_element_type=F32)
            m = m_new
        finish(acc)


def _attention(q, k, v, q_gain, k_gain):
    bsz, seq, _ = q.shape
    g = N_HEADS // N_KV_HEADS
    qh = q.reshape(bsz, seq, N_HEADS, HEAD_DIM).transpose(0, 2, 1, 3)
    kh = k.reshape(bsz, seq, N_KV_HEADS, HEAD_DIM).transpose(0, 2, 1, 3)
    vt = v.reshape(bsz, seq, N_KV_HEADS, HEAD_DIM).transpose(0, 2, 3, 1)
    vt = jnp.concatenate([vt, jnp.ones((bsz, N_KV_HEADS, ONES_ROWS, seq), BF16)], axis=2)
    rows = HEAD_DIM + ONES_ROWS
    bound = HEAD_DIM * Q_SCALE * BF16_NORM_SLACK * jnp.max(jnp.abs(q_gain)) * jnp.max(jnp.abs(k_gain))
    bounded = (bound <= SAFE_SCORE_BOUND).astype(I32).reshape(1)
    tq = Q_TILE
    grid_spec = pltpu.PrefetchScalarGridSpec(
        num_scalar_prefetch=1,
        grid=(bsz, N_KV_HEADS, seq // tq),
        in_specs=[pl.BlockSpec((1, g, tq, HEAD_DIM), lambda b, h, i, f: (b, h, i, 0)),
                  pl.BlockSpec((1, 1, seq, HEAD_DIM), lambda b, h, i, f: (b, h, 0, 0)),
                  pl.BlockSpec((1, 1, rows, seq), lambda b, h, i, f: (b, h, 0, 0))],
        out_specs=pl.BlockSpec((1, g, tq, HEAD_DIM), lambda b, h, i, f: (b, h, i, 0)))
    o = pl.pallas_call(
        _attn_kernel,
        grid_spec=grid_spec,
        out_shape=jax.ShapeDtypeStruct((bsz, N_HEADS, seq, HEAD_DIM), BF16),
        compiler_params=_params("parallel", "parallel", "parallel"),
        name="attention",
    )(bounded, qh, kh, vt)
    return o.transpose(0, 2, 1, 3).reshape(bsz, seq, ATT_WIDTH)


def _merge_kernel(x_ref, yhyT_ref, yat_ref, gate_ref, mod_ref, g_ref, why_ref, wat_ref, wout_ref, wrT_ref,
                  x1_ref, h2_ref, affT_ref):
    d = x_ref.shape[2]
    yhy = yhyT_ref[0].T.astype(BF16)
    a = jnp.dot(yhy, why_ref[...], preferred_element_type=F32)
    b = jnp.dot(yat_ref[0], wat_ref[...], preferred_element_type=F32)
    gate = gate_ref[0]
    mrg = gate[:, :d] * a + gate[:, d:] * b
    o = jnp.dot(mrg.astype(BF16), wout_ref[...], preferred_element_type=F32)
    gt1 = mod_ref[0, 2:3, :]
    sh2 = mod_ref[0, 3:4, :]
    sc2 = mod_ref[0, 4:5, :]
    x1 = x_ref[0] + gt1 * o
    x1_ref[0] = x1
    ms = jnp.mean(x1 * x1, axis=-1, keepdims=True)
    h2 = (x1 * lax.rsqrt(ms + EPS)) * g_ref[...] * (1.0 + sc2) + sh2
    h2_ref[0] = h2.astype(BF16)
    logT = lax.dot_general(wrT_ref[...], h2, NT_DIMS, precision=HIGHEST, preferred_element_type=F32)
    mx = jnp.max(logT, axis=0, keepdims=True)
    ex = jnp.exp(logT - mx)
    affT_ref[0] = ex / jnp.sum(ex, axis=0, keepdims=True)


def _merge(x, yhyT, yat, gates, mod3, g_ffn, w_hy_out, w_att_out, w_out, w_router):
    bsz, seq, d = x.shape
    tm = TOKEN_TILE
    ne = w_router.shape[1]
    full = lambda shape: pl.BlockSpec(shape, lambda b, i: (0,) * len(shape))
    return pl.pallas_call(
        _merge_kernel,
        grid=(bsz, seq // tm),
        in_specs=[pl.BlockSpec((1, tm, d), lambda b, i: (b, i, 0)),
                  pl.BlockSpec((1, HY_WIDTH, tm), lambda b, i: (b, 0, i)),
                  pl.BlockSpec((1, tm, ATT_WIDTH), lambda b, i: (b, i, 0)),
                  pl.BlockSpec((1, tm, 2 * d), lambda b, i: (b, i, 0)),
                  pl.BlockSpec((1, 6, d), lambda b, i: (b, 0, 0)),
                  full((1, d)), full((HY_WIDTH, d)), full((ATT_WIDTH, d)), full((d, d)), full((ne, d))],
        out_specs=[pl.BlockSpec((1, tm, d), lambda b, i: (b, i, 0)),
                   pl.BlockSpec((1, tm, d), lambda b, i: (b, i, 0)),
                   pl.BlockSpec((1, ne, tm), lambda b, i: (b, 0, i))],
        out_shape=[jax.ShapeDtypeStruct((bsz, seq, d), F32),
                   jax.ShapeDtypeStruct((bsz, seq, d), BF16),
                   jax.ShapeDtypeStruct((bsz, ne, seq), F32)],
        compiler_params=_params("parallel", "parallel"),
        name="merge",
    )(x, yhyT, yat, gates, mod3, g_ffn.reshape(1, d), w_hy_out.astype(BF16), w_att_out.astype(BF16),
      w_out.astype(BF16), w_router.T)


def _topk_kernel(a_ref, ut_ref, pos_ref, *, cap):
    nr, seq = a_ref.shape
    a = a_ref[...]

    def count(mask):
        return jnp.sum(mask.astype(F32), axis=1, keepdims=True)

    def probe(t, lo, hi):
        ok = count(a >= t) >= cap
        return ok, jnp.where(ok, t, lo), jnp.where(ok, hi, t)

    def by_exponent(_, c):
        elo, ehi, lo, hi = c
        emid = 0.5 * (elo + ehi)
        ok, lo, hi = probe(jnp.exp2(emid), lo, hi)
        return jnp.where(ok, emid, elo), jnp.where(ok, ehi, emid), lo, hi

    def by_value(_, c):
        lo, hi = c
        _, lo, hi = probe(0.5 * (lo + hi), lo, hi)
        return lo, hi

    col = lambda v: jnp.full((nr, 1), v, F32)
    _, _, lo, hi = lax.fori_loop(0, TOPK_EXP_STEPS, by_exponent, (col(F32_MIN_EXP), col(1.0), col(0.0), col(2.0)))
    lo, hi = lax.fori_loop(0, TOPK_VALUE_STEPS, by_value, (lo, hi))
    gt = a >= hi
    eq = (a >= lo) & (a < hi)
    need = cap - count(gt)
    ut = ut_ref[...]
    run_eq = jnp.zeros((nr, 1), F32)
    run_sel = jnp.zeros((nr, 1), F32)
    for ch in range(seq // LANES):
        sl = slice(ch * LANES, (ch + 1) * LANES)
        eq_rank = jnp.dot(eq[:, sl].astype(BF16), ut, preferred_element_type=F32) + run_eq
        run_eq = eq_rank[:, LANES - 1:LANES]
        sel = gt[:, sl] | (eq[:, sl] & (eq_rank <= need))
        pos = jnp.dot(sel.astype(BF16), ut, preferred_element_type=F32) + run_sel
        run_sel = pos[:, LANES - 1:LANES]
        pos_ref[:, sl] = jnp.where(sel, pos.astype(I32) - 1, -1)


def _topk(aff2, cap):
    nr, seq = aff2.shape
    ut = jnp.asarray(np.triu(np.ones((LANES, LANES), np.float32))).astype(BF16)
    return pl.pallas_call(
        functools.partial(_topk_kernel, cap=cap),
        grid=(1,),
        in_specs=[pl.BlockSpec((nr, seq), lambda i: (0, 0)), pl.BlockSpec((LANES, LANES), lambda i: (0, 0))],
        out_specs=pl.BlockSpec((nr, seq), lambda i: (0, 0)),
        out_shape=jax.ShapeDtypeStruct((nr, seq), I32),
        compiler_params=_params("arbitrary"),
        name="topk",
    )(aff2, ut)


def _window_start(st_ref, idx, w, cap):
    base = st_ref[idx] + w * WINDOW
    return base, pl.multiple_of(jnp.minimum(base, cap - WINDOW), BF16_SUBLANES)


def _gather_kernel(st_ref, np_ref, pos_ref, h2_ref, xe_ref):
    eh, tile = pos_ref.shape[1:]
    cap = xe_ref.shape[2]
    nt = pl.num_programs(2)
    b, half, i = pl.program_id(0), pl.program_id(1), pl.program_id(2)
    ne = pl.num_programs(1) * eh

    @pl.when(i == 0)
    def _():
        xe_ref[...] = jnp.zeros_like(xe_ref)

    rho = lax.broadcasted_iota(I32, (WINDOW, tile), 0)

    def one_pass(w, carry):
        starts, rows = [], []
        for el in range(eh):
            base, st = _window_start(st_ref, (b * ne + half * eh + el) * nt + i, w, cap)
            pos = pos_ref[0, el:el + 1, :]
            rows.append(((pos - st == rho) & (pos >= base)).astype(BF16))
            starts.append(st)
        got = jnp.dot(jnp.concatenate(rows, axis=0), h2_ref[0], preferred_element_type=F32).astype(BF16)
        for el in range(eh):
            xe_ref[0, el, pl.ds(starts[el], WINDOW), :] += got[el * WINDOW:(el + 1) * WINDOW]
        return carry

    lax.fori_loop(0, np_ref[(b * pl.num_programs(1) + half) * nt + i], one_pass, 0)


def _gather(st16, npass_half, pos, h2, cap):
    bsz, ne, seq = pos.shape
    d = h2.shape[2]
    tile = TOKEN_TILE
    eh = ne // GATHER_SPLIT
    grid_spec = pltpu.PrefetchScalarGridSpec(
        num_scalar_prefetch=2,
        grid=(bsz, GATHER_SPLIT, seq // tile),
        in_specs=[pl.BlockSpec((1, eh, tile), lambda b, h, i, st, n: (b, h, i)),
                  pl.BlockSpec((1, tile, d), lambda b, h, i, st, n: (b, i, 0))],
        out_specs=pl.BlockSpec((1, eh, cap, d), lambda b, h, i, st, n: (b, h, 0, 0)))
    return pl.pallas_call(
        _gather_kernel,
        grid_spec=grid_spec,
        out_shape=jax.ShapeDtypeStruct((bsz, ne, cap, d), BF16),
        compiler_params=_params("parallel", "parallel", "arbitrary"),
        name="gather",
    )(st16, npass_half, pos, h2)


def _moe_kernel(xe_ref, wg_ref, wu_ref, wd_ref, ye_ref):
    tile = TOKEN_TILE
    for r in range(xe_ref.shape[2] // tile):
        xe = xe_ref[0, 0, r * tile:(r + 1) * tile, :]
        a = jnp.dot(xe, wg_ref[0], preferred_element_type=F32)
        u = jnp.dot(xe, wu_ref[0], preferred_element_type=F32)
        hmid = (a * jax.nn.sigmoid(a) * u).astype(BF16)
        ye_ref[0, 0, r * tile:(r + 1) * tile, :] = jnp.dot(hmid, wd_ref[0],
                                                          preferred_element_type=F32).astype(BF16)


def _moe(xe, wg, wu, wd):
    bsz, ne, cap, d = xe.shape
    dff = wg.shape[2]
    return pl.pallas_call(
        _moe_kernel,
        grid=(bsz, ne),
        in_specs=[pl.BlockSpec((1, 1, cap, d), lambda b, e: (b, e, 0, 0)),
                  pl.BlockSpec((1, d, dff), lambda b, e: (e, 0, 0)),
                  pl.BlockSpec((1, d, dff), lambda b, e: (e, 0, 0)),
                  pl.BlockSpec((1, dff, d), lambda b, e: (e, 0, 0))],
        out_specs=pl.BlockSpec((1, 1, cap, d), lambda b, e: (b, e, 0, 0)),
        out_shape=jax.ShapeDtypeStruct((bsz, ne, cap, d), BF16),
        compiler_params=_params("parallel", "parallel"),
        name="moe",
    )(xe, wg, wu, wd)


def _combine_kernel(st_ref, np_ref, pos_ref, aff_ref, ye_ref, x1_ref, mod_ref, ex_ref, rho_ref, o_ref,
                    rhs_ref, acc_ref):
    tile, ne = pos_ref.shape[1:]
    cap = ye_ref.shape[2]
    nt = pl.num_programs(1)
    b, i = pl.program_id(0), pl.program_id(1)
    acc_ref[...] = jnp.zeros_like(acc_ref)
    lane = lax.broadcasted_iota(I32, (1, ne), 1)
    pos = pos_ref[0]
    aexp = jnp.dot(aff_ref[0].astype(BF16), ex_ref[...], preferred_element_type=F32)

    def one_pass(w, carry):
        base_v = jnp.zeros((1, ne), I32)
        st_v = jnp.zeros((1, ne), I32)
        for e in range(ne):
            base, st = _window_start(st_ref, (b * ne + e) * nt + i, w, cap)
            rhs_ref[e * WINDOW:(e + 1) * WINDOW, :] = ye_ref[0, e, pl.ds(st, WINDOW), :]
            base_v = jnp.where(lane == e, base, base_v)
            st_v = jnp.where(lane == e, st, st_v)
        off = pos - st_v
        off = jnp.where((pos >= base_v) & (off >= 0) & (off < WINDOW), off, -1)
        oexp = jnp.dot(off.astype(F32).astype(BF16), ex_ref[...], preferred_element_type=F32)
        pt = jnp.where(oexp == rho_ref[...], aexp, 0.0).astype(BF16)
        acc_ref[...] += jnp.dot(pt, rhs_ref[...], preferred_element_type=F32)
        return carry

    lax.fori_loop(0, np_ref[b * nt + i], one_pass, 0)
    o_ref[0] = x1_ref[0] + mod_ref[0, 5:6, :] * acc_ref[...]


def _combine(st16, npass, pos_tok, aff_tok, ye, x1, mod3):
    bsz, seq, d = x1.shape
    ne, cap = ye.shape[1:3]
    tile = TOKEN_TILE
    spread = np.kron(np.eye(ne, dtype=np.float32), np.ones((1, WINDOW), np.float32))
    rho = np.tile(np.arange(WINDOW, dtype=np.float32), ne).reshape(1, ne * WINDOW)
    grid_spec = pltpu.PrefetchScalarGridSpec(
        num_scalar_prefetch=2,
        grid=(bsz, seq // tile),
        in_specs=[pl.BlockSpec((1, tile, ne), lambda b, i, st, n: (b, i, 0)),
                  pl.BlockSpec((1, tile, ne), lambda b, i, st, n: (b, i, 0)),
                  pl.BlockSpec((1, ne, cap, d), lambda b, i, st, n: (b, 0, 0, 0), pipeline_mode=pl.Buffered(1)),
                  pl.BlockSpec((1, tile, d), lambda b, i, st, n: (b, i, 0)),
                  pl.BlockSpec((1, 6, d), lambda b, i, st, n: (b, 0, 0)),
                  pl.BlockSpec((ne, ne * WINDOW), lambda b, i, st, n: (0, 0)),
                  pl.BlockSpec((1, ne * WINDOW), lambda b, i, st, n: (0, 0))],
        out_specs=pl.BlockSpec((1, tile, d), lambda b, i, st, n: (b, i, 0)),
        scratch_shapes=[pltpu.VMEM((ne * WINDOW, d), BF16), pltpu.VMEM((tile, d), F32)])
    return pl.pallas_call(
        _combine_kernel,
        grid_spec=grid_spec,
        out_shape=jax.ShapeDtypeStruct((bsz, seq, d), F32),
        compiler_params=_params("parallel", "arbitrary"),
        name="combine",
    )(st16, npass, pos_tok, aff_tok, ye, x1, mod3, jnp.asarray(spread).astype(BF16), jnp.asarray(rho))


def _route_tables(pos):
    bsz, ne, seq = pos.shape
    tile = TOKEN_TILE
    counts = jnp.sum((pos >= 0).reshape(bsz, ne, seq // tile, tile), axis=-1, dtype=I32)
    starts = jnp.cumsum(counts, axis=-1) - counts
    st16 = (starts // BF16_SUBLANES) * BF16_SUBLANES
    span = jnp.where(counts > 0, starts - st16 + counts, 0)
    passes = (span + WINDOW - 1) // WINDOW
    npass = jnp.max(passes, axis=1)
    npass_half = jnp.max(passes.reshape(bsz, GATHER_SPLIT, ne // GATHER_SPLIT, -1), axis=2)
    return st16.reshape(-1), npass.reshape(-1), npass_half.reshape(-1)


def kernel(x, c, w_ada, b_ada, g_mix, g_ffn, w_in, b_in, short_w, short_b, hy_w1, hy_b1, hy_w2, hy_b2, hy_w3, hy_b3, hy_w4, hy_freq, hy_bias, q_gain, k_gain, w_hy_out, w_att_out, w_out, w_router, w_gate, w_up, w_down):
    bsz, seq, d = x.shape
    depth = w_ada.shape[0]
    ne = w_router.shape[-1]
    cap = EC_FACTOR * seq // ne
    for l in range(depth):
        mod3 = _adaln(c, w_ada[l], b_ada[l]).reshape(bsz, 6, d)
        zhyT, q, k, v, gates = _inproj(x, mod3, g_mix[l], w_in[l], b_in[l], q_gain[l], k_gain[l])
        hT = _filters(seq, hy_w1[l], hy_b1[l], hy_w2[l], hy_b2[l], hy_w3[l], hy_b3[l], hy_w4[l], hy_freq[l])
        yhyT = _hyena(zhyT, hT, short_w[l], short_b[l], hy_bias[l])
        yat = _attention(q, k, v, q_gain[l], k_gain[l])
        x1, h2, affT = _merge(x, yhyT, yat, gates, mod3, g_ffn[l], w_hy_out[l], w_att_out[l], w_out[l],
                              w_router[l])
        pos = _topk(affT.reshape(bsz * ne, seq), cap).reshape(bsz, ne, seq)
        st16, npass, npass_half = _route_tables(pos)
        xe = _gather(st16, npass_half, pos, h2, cap)
        ye = _moe(xe, w_gate[l].astype(BF16), w_up[l].astype(BF16), w_down[l].astype(BF16))
        x = _combine(st16, npass, pos.transpose(0, 2, 1), affT.transpose(0, 2, 1), ye, x1, mod3)
    return x
```

```python
import functools
import math

import numpy as np
import jax
import jax.numpy as jnp
from jax import lax
from jax.experimental import pallas as pl
from jax.experimental.pallas import tpu as pltpu

F32 = jnp.float32
BF16 = jnp.bfloat16
I32 = jnp.int32
HIGHEST = lax.Precision.HIGHEST

EPS = 1e-6
GRID_W = 64
HY_WIDTH = 512
HY_ORDER = 2
SHORT_K = 3
FILTER_EMB = 33
FILTER_HIDDEN = 64
HY_QUICK_DECAY_PCT = 0.3
HY_GRADUAL_DECAY_PCT = 1.5
HY_DECAY_TARGET = 1e-2
N_HEADS = 8
N_KV_HEADS = 2
HEAD_DIM = 64
ROPE_THETA = 10000.0
N_EXPERTS = 16
EC_FACTOR = 2
HY_IN = 3 * HY_WIDTH
ATT_WIDTH = N_HEADS * HEAD_DIM
KV_WIDTH = N_KV_HEADS * HEAD_DIM

LANES = 128
VMEM_LIMIT = 56 * 1024 * 1024
MXU_TILE = 256
TOKEN_TILE = 256
Q_TILE = 256
KV_CHUNK = 1024
HY_CH_BLOCK = 8
HY_CH_BATCH = 4
FILTER_ROWS = 128
WINDOW = 64
BF16_SUBLANES = 16
GATHER_SPLIT = 2
SAFE_SCORE_BOUND = 48.0
BF16_NORM_SLACK = 1.01
F32_MIN_EXP = -150.0
TOPK_EXP_STEPS = 10
TOPK_VALUE_STEPS = 28
ONES_ROWS = 16
Q_SCALE = HEAD_DIM ** -0.5 * math.log2(math.e)

NT_DIMS = (((1,), (1,)), ((), ()))


def _params(*sem):
    return pltpu.CompilerParams(dimension_semantics=sem, vmem_limit_bytes=VMEM_LIMIT)


def _split2(a):
    hi = a.astype(BF16)
    return hi, (a - hi.astype(F32)).astype(BF16)


def _dot_split_lhs(a, b_bf16):
    hi, lo = _split2(a)
    return (jnp.dot(hi, b_bf16, preferred_element_type=F32)
            + jnp.dot(lo, b_bf16, preferred_element_type=F32))


def _adaln_kernel(c_ref, w_ref, b_ref, o_ref):
    c = c_ref[...]
    s = c * jax.nn.sigmoid(c)
    o_ref[...] = jnp.dot(s, w_ref[...], precision=HIGHEST, preferred_element_type=F32) + b_ref[...]


def _adaln(c, w, b):
    bsz, d = c.shape
    n = w.shape[1]
    rows = 8
    cp = jnp.zeros((rows, d), F32).at[:bsz].set(c)
    tn = 1536
    out = pl.pallas_call(
        _adaln_kernel,
        grid=(n // tn,),
        in_specs=[pl.BlockSpec((rows, d), lambda j: (0, 0)),
                  pl.BlockSpec((d, tn), lambda j: (0, j)),
                  pl.BlockSpec((1, tn), lambda j: (0, j))],
        out_specs=pl.BlockSpec((rows, tn), lambda j: (0, j)),
        out_shape=jax.ShapeDtypeStruct((rows, n), F32),
        compiler_params=_params("parallel"),
        name="adaln",
    )(cp, w, b.reshape(1, n))
    return out[:bsz]


def _inproj_kernel(x_ref, mod_ref, g_ref, whyT_ref, bhy_ref, wr_ref, br_ref, cos_ref, sin_ref,
                   qg_ref, kg_ref, bd_ref, zhyT_ref, q_ref, k_ref, v_ref, gate_ref):
    x = x_ref[0]
    ms = jnp.mean(x * x, axis=-1, keepdims=True)
    sh1 = mod_ref[0, 0:1, :]
    sc1 = mod_ref[0, 1:2, :]
    h = (x * lax.rsqrt(ms + EPS)) * g_ref[...] * (1.0 + sc1) + sh1
    hb = h.astype(BF16)
    zhyT_ref[0] = lax.dot_general(whyT_ref[...], hb, NT_DIMS, preferred_element_type=F32) + bhy_ref[...]
    zr = jnp.dot(hb, wr_ref[...], preferred_element_type=F32) + br_ref[...]

    cos = cos_ref[...]
    sin = sin_ref[...]
    bd = bd_ref[...]
    lane = lax.broadcasted_iota(I32, cos.shape, 1)
    first = (lane & (HEAD_DIM // 2 - 1)) < HEAD_DIM // 4

    def norm_rope(u, gain, scale):
        ss = _dot_split_lhs(u * u, bd)
        un = (u * lax.rsqrt(ss * (1.0 / HEAD_DIM) + EPS)) * gain
        rot = jnp.where(first, pltpu.roll(un, LANES - HEAD_DIM // 4, 1), pltpu.roll(un, HEAD_DIM // 4, 1))
        return (un * cos + rot * sin) * scale

    for j in range(ATT_WIDTH // LANES):
        u = zr[:, j * LANES:(j + 1) * LANES]
        q_ref[0, :, j * LANES:(j + 1) * LANES] = norm_rope(u, qg_ref[...], Q_SCALE).astype(BF16)
    k_ref[0] = norm_rope(zr[:, ATT_WIDTH:ATT_WIDTH + KV_WIDTH], kg_ref[...], 1.0).astype(BF16)
    v_ref[0] = zr[:, ATT_WIDTH + KV_WIDTH:ATT_WIDTH + 2 * KV_WIDTH].astype(BF16)
    gate_ref[0] = jax.nn.sigmoid(zr[:, ATT_WIDTH + 2 * KV_WIDTH:])


def _rope_tables(seq):
    rows = seq // GRID_W
    t = np.arange(seq)
    row = (t // GRID_W).astype(np.float32)
    col = (t % GRID_W).astype(np.float32)
    half = HEAD_DIM // 2
    quarter = half // 2
    inv = (ROPE_THETA ** (-np.arange(0, half, 2, dtype=np.float32) / half)).astype(np.float32)
    ang_r = row[:, None] * inv[None, :]
    ang_c = col[:, None] * inv[None, :]
    ang = np.concatenate([ang_r, ang_r, ang_c, ang_c], axis=1)
    sign = np.concatenate([-np.ones(quarter), np.ones(quarter)] * 2).astype(np.float32)
    cos = np.cos(ang).astype(np.float32)
    sin = (np.sin(ang) * sign[None, :]).astype(np.float32)
    reps = LANES // HEAD_DIM
    del rows
    return jnp.asarray(np.tile(cos, (1, reps))), jnp.asarray(np.tile(sin, (1, reps)))


def _inproj(x, mod3, g_mix, w_in, b_in, q_gain, k_gain):
    bsz, seq, d = x.shape
    tm = TOKEN_TILE
    n_rest = w_in.shape[1] - HY_IN
    n_gate = n_rest - ATT_WIDTH - 2 * KV_WIDTH
    whyT = w_in[:, :HY_IN].T.astype(BF16)
    wr = w_in[:, HY_IN:].astype(BF16)
    bhy = b_in[:HY_IN].reshape(HY_IN, 1)
    br = b_in[HY_IN:].reshape(1, n_rest)
    cos, sin = _rope_tables(seq)
    reps = LANES // HEAD_DIM
    qg = jnp.tile(q_gain, reps).reshape(1, LANES)
    kg = jnp.tile(k_gain, reps).reshape(1, LANES)
    bd = jnp.asarray(np.kron(np.eye(reps, dtype=np.float32), np.ones((HEAD_DIM, HEAD_DIM), np.float32))).astype(BF16)
    full = lambda shape: pl.BlockSpec(shape, lambda b, i: (0,) * len(shape))
    return pl.pallas_call(
        _inproj_kernel,
        grid=(bsz, seq // tm),
        in_specs=[pl.BlockSpec((1, tm, d), lambda b, i: (b, i, 0)),
                  pl.BlockSpec((1, 6, d), lambda b, i: (b, 0, 0)),
                  full((1, d)), full((HY_IN, d)), full((HY_IN, 1)), full((d, n_rest)), full((1, n_rest)),
                  pl.BlockSpec((tm, LANES), lambda b, i: (i, 0)),
                  pl.BlockSpec((tm, LANES), lambda b, i: (i, 0)),
                  full((1, LANES)), full((1, LANES)), full((LANES, LANES))],
        out_specs=[pl.BlockSpec((1, HY_IN, tm), lambda b, i: (b, 0, i)),
                   pl.BlockSpec((1, tm, ATT_WIDTH), lambda b, i: (b, i, 0)),
                   pl.BlockSpec((1, tm, KV_WIDTH), lambda b, i: (b, i, 0)),
                   pl.BlockSpec((1, tm, KV_WIDTH), lambda b, i: (b, i, 0)),
                   pl.BlockSpec((1, tm, n_gate), lambda b, i: (b, i, 0))],
        out_shape=[jax.ShapeDtypeStruct((bsz, HY_IN, seq), F32),
                   jax.ShapeDtypeStruct((bsz, seq, ATT_WIDTH), BF16),
                   jax.ShapeDtypeStruct((bsz, seq, KV_WIDTH), BF16),
                   jax.ShapeDtypeStruct((bsz, seq, KV_WIDTH), BF16),
                   jax.ShapeDtypeStruct((bsz, seq, n_gate), F32)],
        compiler_params=_params("parallel", "parallel"),
        name="inproj",
    )(x, mod3, g_mix.reshape(1, d), whyT, bhy, wr, br, cos, sin, qg, kg, bd)


def _filter_kernel(zT_ref, w1T_ref, b1_ref, w2T_ref, b2_ref, w3T_ref, b3_ref, fr_ref, w4T_ref,
                   t01_ref, absd_ref, o_ref, hdn_ref):
    @pl.when(pl.program_id(0) == 0)
    def _():
        fr = fr_ref[...]
        h = jnp.sin(fr[:, 0:1] * (jnp.dot(w1T_ref[...], zT_ref[...], precision=HIGHEST,
                                          preferred_element_type=F32) + b1_ref[...]))
        h = jnp.sin(fr[:, 1:2] * (jnp.dot(w2T_ref[...], h, precision=HIGHEST,
                                          preferred_element_type=F32) + b2_ref[...]))
        h = jnp.sin(fr[:, 2:3] * (jnp.dot(w3T_ref[...], h, precision=HIGHEST,
                                          preferred_element_type=F32) + b3_ref[...]))
        hdn_ref[...] = h

    hT = jnp.dot(w4T_ref[...], hdn_ref[...], precision=HIGHEST, preferred_element_type=F32)
    hT = hT * jnp.exp(-t01_ref[...] * absd_ref[...])
    nrm = jnp.sum(jnp.abs(hT), axis=1, keepdims=True) + EPS
    o_ref[...] = hT / nrm


def _filters(seq, w1, b1, w2, b2, w3, b3, w4, freq):
    emb_pad = 64
    bands = (FILTER_EMB - 1) // 2
    t01 = np.linspace(0.0, 1.0, seq, dtype=np.float32)[None, :]
    f = np.linspace(1e-4, bands - 1, bands, dtype=np.float32)[:, None]
    w = ((2.0 * math.pi) * np.arange(seq, dtype=np.float32) / seq).astype(np.float32)[None, :]
    zT = np.zeros((emb_pad, seq), np.float32)
    zT[0:1] = t01
    zT[1:1 + bands] = np.cos(f * w)
    zT[1 + bands:1 + 2 * bands] = -np.sin(f * w)
    w1T = jnp.zeros((FILTER_HIDDEN, emb_pad), F32).at[:, :FILTER_EMB].set(w1.T)
    max_decay = math.log(HY_DECAY_TARGET) / HY_QUICK_DECAY_PCT
    min_decay = math.log(HY_DECAY_TARGET) / HY_GRADUAL_DECAY_PCT
    deltas = np.abs(np.linspace(min_decay, max_decay, HY_WIDTH, dtype=np.float32))
    n_rows = HY_ORDER * 2 * HY_WIDTH
    absd = np.tile(deltas, HY_ORDER * 2).reshape(n_rows, 1)
    rb = FILTER_ROWS
    full = lambda shape: pl.BlockSpec(shape, lambda j: (0,) * len(shape))
    col = lambda v: v.reshape(FILTER_HIDDEN, 1)
    return pl.pallas_call(
        _filter_kernel,
        grid=(n_rows // rb,),
        in_specs=[full((emb_pad, seq)),
                  full((FILTER_HIDDEN, emb_pad)), full((FILTER_HIDDEN, 1)),
                  full((FILTER_HIDDEN, FILTER_HIDDEN)), full((FILTER_HIDDEN, 1)),
                  full((FILTER_HIDDEN, FILTER_HIDDEN)), full((FILTER_HIDDEN, 1)),
                  full((FILTER_HIDDEN, 3)),
                  pl.BlockSpec((rb, FILTER_HIDDEN), lambda j: (j, 0)),
                  full((1, seq)),
                  pl.BlockSpec((rb, 1), lambda j: (j, 0))],
        out_specs=pl.BlockSpec((rb, seq), lambda j: (j, 0)),
        out_shape=jax.ShapeDtypeStruct((n_rows, seq), F32),
        scratch_shapes=[pltpu.VMEM((FILTER_HIDDEN, seq), F32)],
        compiler_params=_params("arbitrary"),
        name="filters",
    )(jnp.asarray(zT), w1T, col(b1), w2.T, col(b2), w3.T, col(b3), freq.T, w4.T,
      jnp.asarray(t01), jnp.asarray(absd))


def _dft_tables(seq):
    n = 2 * seq
    r = int(round(math.sqrt(n)))
    assert r * r == n, "2*seq must be a perfect square"
    hh = r // 2
    k = np.arange(r, dtype=np.float64)
    ang = 2.0 * np.pi * np.outer(k, k) / r
    fr, fi = np.cos(ang), -np.sin(ang)
    angt = 2.0 * np.pi * np.outer(k, k) / n
    twr, twi = np.cos(angt), -np.sin(angt)
    f1 = np.concatenate([fr[:, :hh], fi[:, :hh]], axis=0)
    f3 = np.block([[fr, fi], [-fi, fr]])
    f3i = np.block([[fr, -fi], [fi, fr]])
    f1i = np.concatenate([fr[:hh, :], fi[:hh, :]], axis=1) / n
    as32 = lambda a: jnp.asarray(a.astype(np.float32))
    return as32(f1), as32(twr), as32(twi), as32(f3), as32(f3i), as32(f1i)


def _hyena_kernel(sw_ref, sb_ref, fb_ref, v_ref, x1_ref, x2_ref, h_ref, f1_f32, twr_ref, twi_ref,
                  f3_f32, f3i_f32, f1i_f32, o_ref, f1_ref, f3_ref, f3i_ref, f1i_ref):
    bsz, cb, hh, r = v_ref.shape
    j = pl.program_id(0)
    f1_ref[...] = f1_f32[...].astype(BF16)
    f3_ref[...] = f3_f32[...].astype(BF16)
    f3i_ref[...] = f3i_f32[...].astype(BF16)
    f1i_ref[...] = f1i_f32[...].astype(BF16)
    twr = twr_ref[...]
    twi = twi_ref[...]
    lane = lax.broadcasted_iota(I32, (hh, r), 1)
    row = lax.broadcasted_iota(I32, (hh, r), 0)

    def shift_prev(z):
        a = pltpu.roll(z, 1, 1)
        a = jnp.where(lane == 0, pltpu.roll(a, 1, 0), a)
        return jnp.where((lane == 0) & (row == 0), 0.0, a)

    def shift_next(z):
        a = pltpu.roll(z, r - 1, 1)
        a = jnp.where(lane == r - 1, pltpu.roll(a, hh - 1, 0), a)
        return jnp.where((lane == r - 1) & (row == hh - 1), 0.0, a)

    def fwd(slabs):
        m = len(slabs)
        acat = jnp.concatenate([a.astype(BF16) for a in slabs], axis=1)
        s = jnp.dot(f1_ref[...], acat, preferred_element_type=F32)
        lhs = []
        for i in range(m):
            br = s[:r, i * r:(i + 1) * r]
            bi = s[r:, i * r:(i + 1) * r]
            cr = br * twr - bi * twi
            ci = br * twi + bi * twr
            lhs.append(jnp.concatenate([cr, ci], axis=1).astype(BF16))
        d = jnp.dot(jnp.concatenate(lhs, axis=0), f3_ref[...], preferred_element_type=F32)
        return [(d[i * r:(i + 1) * r, :r], d[i * r:(i + 1) * r, r:]) for i in range(m)]

    def inv(specs):
        m = len(specs)
        lhs = jnp.concatenate([jnp.concatenate([pr, pi], axis=1).astype(BF16) for pr, pi in specs], axis=0)
        e = jnp.dot(lhs, f3i_ref[...], preferred_element_type=F32)
        rhs = []
        for i in range(m):
            er = e[i * r:(i + 1) * r, :r]
            ei = e[i * r:(i + 1) * r, r:]
            tr = er * twr + ei * twi
            ti = ei * twr - er * twi
            rhs.append(jnp.concatenate([tr, ti], axis=0).astype(BF16))
        y = jnp.dot(f1i_ref[...], jnp.concatenate(rhs, axis=1), preferred_element_type=F32)
        return [y[:, i * r:(i + 1) * r] for i in range(m)]

    def cmul(a, g):
        return a[0] * g[0] - a[1] * g[1], a[0] * g[1] + a[1] * g[0]

    def sconv(ref, b, c, gi):
        z = ref[b, c]
        return (sb_ref[gi] + sw_ref[gi] * shift_prev(z) + sw_ref[HY_IN + gi] * z
                + sw_ref[2 * HY_IN + gi] * shift_next(z))

    nb = HY_CH_BATCH

    def body(t, carry):
        cs = [t * nb + i for i in range(nb)]
        chs = [j * cb + c for c in cs]
        pairs = [(c, ch, b) for c, ch in zip(cs, chs) for b in range(bsz)]
        v = [sconv(v_ref, b, c, ch) for c, ch, b in pairs]
        filt = [h_ref[o, d, c] for c in cs for o in range(HY_ORDER) for d in range(2)]
        spec = fwd(v + filt)
        nv = len(v)
        g = []
        for i in range(nb):
            gi = []
            for o in range(HY_ORDER):
                hf = spec[nv + (i * HY_ORDER + o) * 2]
                hb = spec[nv + (i * HY_ORDER + o) * 2 + 1]
                gi.append((hf[0] + hb[0], hf[1] - hb[1]))
            g.append(gi)
        conv = inv([cmul(spec[n], g[n // bsz][0]) for n in range(nv)])
        y = [sconv(x1_ref, b, c, HY_WIDTH + ch) * (conv[n] + fb_ref[ch] * v[n])
             for n, (c, ch, b) in enumerate(pairs)]
        spec = fwd(y)
        conv = inv([cmul(spec[n], g[n // bsz][1]) for n in range(nv)])
        for n, (c, ch, b) in enumerate(pairs):
            o_ref[b, c] = (sconv(x2_ref, b, c, 2 * HY_WIDTH + ch)
                           * (conv[n] + fb_ref[HY_WIDTH + ch] * y[n]))
        return carry

    lax.fori_loop(0, cb // nb, body, 0)


def _hyena(zhyT, hT, short_w, short_b, filt_bias):
    bsz, _, seq = zhyT.shape
    tabs = _dft_tables(seq)
    r = tabs[1].shape[0]
    hh = r // 2
    z4 = zhyT.reshape(bsz, HY_IN, hh, r)
    h5 = hT.reshape(HY_ORDER, 2, HY_WIDTH, hh, r)
    cb = HY_CH_BLOCK
    nblk = HY_WIDTH // cb
    smem = pl.BlockSpec(memory_space=pltpu.SMEM)
    full = lambda a: pl.BlockSpec(a.shape, lambda j: (0,) * a.ndim)
    zspec = lambda off: pl.BlockSpec((bsz, cb, hh, r), lambda j: (0, off * nblk + j, 0, 0))
    out = pl.pallas_call(
        _hyena_kernel,
        grid=(nblk,),
        in_specs=[smem, smem, smem, zspec(0), zspec(1), zspec(2),
                  pl.BlockSpec((HY_ORDER, 2, cb, hh, r), lambda j: (0, 0, j, 0, 0))]
                 + [full(t) for t in tabs],
        out_specs=pl.BlockSpec((bsz, cb, hh, r), lambda j: (0, j, 0, 0)),
        out_shape=jax.ShapeDtypeStruct((bsz, HY_WIDTH, hh, r), F32),
        scratch_shapes=[pltpu.VMEM(tabs[i].shape, BF16) for i in (0, 3, 4, 5)],
        compiler_params=_params("parallel"),
        name="hyena",
    )(short_w.reshape(-1), short_b, filt_bias.reshape(-1), z4, z4, z4, h5, *tabs)
    return out.reshape(bsz, HY_WIDTH, seq)


def _attn_kernel(bounded_ref, q_ref, k_ref, vt_ref, o_ref):
    g, tq, hd = q_ref.shape[1:]
    seq = k_ref.shape[2]
    rows = vt_ref.shape[2]
    n_chunks = seq // KV_CHUNK
    q = q_ref[0].reshape(g * tq, hd)

    def scores(kc):
        kb = k_ref[0, 0, kc * KV_CHUNK:(kc + 1) * KV_CHUNK, :]
        return lax.dot_general(kb, q, NT_DIMS, preferred_element_type=F32)

    def values(kc):
        return vt_ref[0, 0, :, kc * KV_CHUNK:(kc + 1) * KV_CHUNK]

    def finish(acc):
        out = acc[:hd] / acc[hd:hd + 1]
        o_ref[0] = out.T.reshape(g, tq, hd).astype(o_ref.dtype)

    @pl.when(bounded_ref[0] == 1)
    def _():
        acc = jnp.zeros((rows, g * tq), F32)
        for kc in range(n_chunks):
            p = jnp.exp2(scores(kc)).astype(BF16)
            acc = acc + jnp.dot(values(kc), p, preferred_element_type=F32)
        finish(acc)

    @pl.when(bounded_ref[0] != 1)
    def _():
        m = jnp.full((1, g * tq), -jnp.inf, F32)
        acc = jnp.zeros((rows, g * tq), F32)
        s_next = scores(0)
        for kc in range(n_chunks):
            s = s_next
            if kc + 1 < n_chunks:
                s_next = scores(kc + 1)
            m_new = jnp.maximum(m, jnp.max(s, axis=0, keepdims=True))
            p = jnp.exp2(s - m_new).astype(BF16)
            acc = jnp.exp2(m - m_new) * acc + jnp.dot(values(kc), p, preferred_element_type=F32)
            m = m_new
        finish(acc)


def _attention(q, k, v, q_gain, k_gain):
    bsz, seq, _ = q.shape
    g = N_HEADS // N_KV_HEADS
    qh = q.reshape(bsz, seq, N_HEADS, HEAD_DIM).transpose(0, 2, 1, 3)
    kh = k.reshape(bsz, seq, N_KV_HEADS, HEAD_DIM).transpose(0, 2, 1, 3)
    vt = v.reshape(bsz, seq, N_KV_HEADS, HEAD_DIM).transpose(0, 2, 3, 1)
    vt = jnp.concatenate([vt, jnp.ones((bsz, N_KV_HEADS, ONES_ROWS, seq), BF16)], axis=2)
    rows = HEAD_DIM + ONES_ROWS
    bound = HEAD_DIM * Q_SCALE * BF16_NORM_SLACK * jnp.max(jnp.abs(q_gain)) * jnp.max(jnp.abs(k_gain))
    bounded = (bound <= SAFE_SCORE_BOUND).astype(I32).reshape(1)
    tq = Q_TILE
    grid_spec = pltpu.PrefetchScalarGridSpec(
        num_scalar_prefetch=1,
        grid=(bsz, N_KV_HEADS, seq // tq),
        in_specs=[pl.BlockSpec((1, g, tq, HEAD_DIM), lambda b, h, i, f: (b, h, i, 0)),
                  pl.BlockSpec((1, 1, seq, HEAD_DIM), lambda b, h, i, f: (b, h, 0, 0)),
                  pl.BlockSpec((1, 1, rows, seq), lambda b, h, i, f: (b, h, 0, 0))],
        out_specs=pl.BlockSpec((1, g, tq, HEAD_DIM), lambda b, h, i, f: (b, h, i, 0)))
    o = pl.pallas_call(
        _attn_kernel,
        grid_spec=grid_spec,
        out_shape=jax.ShapeDtypeStruct((bsz, N_HEADS, seq, HEAD_DIM), BF16),
        compiler_params=_params("parallel", "parallel", "parallel"),
        name="attention",
    )(bounded, qh, kh, vt)
    return o.transpose(0, 2, 1, 3).reshape(bsz, seq, ATT_WIDTH)


def _merge_kernel(x_ref, yhyT_ref, yat_ref, gate_ref, mod_ref, g_ref, why_ref, wat_ref, wout_ref, wr_ref,
                  x1_ref, h2_ref, aff_ref):
    d = x_ref.shape[2]
    yhy = yhyT_ref[0].T.astype(BF16)
    a = jnp.dot(yhy, why_ref[...], preferred_element_type=F32)
    b = jnp.dot(yat_ref[0], wat_ref[...], preferred_element_type=F32)
    gate = gate_ref[0]
    mrg = gate[:, :d] * a + gate[:, d:] * b
    o = jnp.dot(mrg.astype(BF16), wout_ref[...], preferred_element_type=F32)
    gt1 = mod_ref[0, 2:3, :]
    sh2 = mod_ref[0, 3:4, :]
    sc2 = mod_ref[0, 4:5, :]
    x1 = x_ref[0] + gt1 * o
    x1_ref[0] = x1
    ms = jnp.mean(x1 * x1, axis=-1, keepdims=True)
    h2 = (x1 * lax.rsqrt(ms + EPS)) * g_ref[...] * (1.0 + sc2) + sh2
    hi, lo = _split2(h2)
    h2_ref[0] = hi
    w_hi, w_lo = _split2(wr_ref[...])
    logits = (jnp.dot(hi, w_hi, preferred_element_type=F32)
              + jnp.dot(lo, w_hi, preferred_element_type=F32)
              + jnp.dot(hi, w_lo, preferred_element_type=F32))
    ex = jnp.exp(logits - jnp.max(logits, axis=-1, keepdims=True))
    aff_ref[0] = ex / jnp.sum(ex, axis=-1, keepdims=True)


def _merge(x, yhyT, yat, gates, mod3, g_ffn, w_hy_out, w_att_out, w_out, w_router):
    bsz, seq, d = x.shape
    tm = TOKEN_TILE
    ne = w_router.shape[1]
    full = lambda shape: pl.BlockSpec(shape, lambda b, i: (0,) * len(shape))
    return pl.pallas_call(
        _merge_kernel,
        grid=(bsz, seq // tm),
        in_specs=[pl.BlockSpec((1, tm, d), lambda b, i: (b, i, 0)),
                  pl.BlockSpec((1, HY_WIDTH, tm), lambda b, i: (b, 0, i)),
                  pl.BlockSpec((1, tm, ATT_WIDTH), lambda b, i: (b, i, 0)),
                  pl.BlockSpec((1, tm, 2 * d), lambda b, i: (b, i, 0)),
                  pl.BlockSpec((1, 6, d), lambda b, i: (b, 0, 0)),
                  full((1, d)), full((HY_WIDTH, d)), full((ATT_WIDTH, d)), full((d, d)), full((d, ne))],
        out_specs=[pl.BlockSpec((1, tm, d), lambda b, i: (b, i, 0)),
                   pl.BlockSpec((1, tm, d), lambda b, i: (b, i, 0)),
                   pl.BlockSpec((1, tm, ne), lambda b, i: (b, i, 0))],
        out_shape=[jax.ShapeDtypeStruct((bsz, seq, d), F32),
                   jax.ShapeDtypeStruct((bsz, seq, d), BF16),
                   jax.ShapeDtypeStruct((bsz, seq, ne), F32)],
        compiler_params=_params("parallel", "parallel"),
        name="merge",
    )(x, yhyT, yat, gates, mod3, g_ffn.reshape(1, d), w_hy_out.astype(BF16), w_att_out.astype(BF16),
      w_out.astype(BF16), w_router)


def _topk_kernel(a_ref, ut_ref, pos_ref, *, cap):
    nr, seq = a_ref.shape
    a = a_ref[...]

    def count(mask):
        return jnp.sum(mask.astype(F32), axis=1, keepdims=True)

    def probe(t, lo, hi):
        ok = count(a >= t) >= cap
        return ok, jnp.where(ok, t, lo), jnp.where(ok, hi, t)

    def by_exponent(_, c):
        elo, ehi, lo, hi = c
        emid = 0.5 * (elo + ehi)
        ok, lo, hi = probe(jnp.exp2(emid), lo, hi)
        return jnp.where(ok, emid, elo), jnp.where(ok, ehi, emid), lo, hi

    def by_value(_, c):
        lo, hi = c
        _, lo, hi = probe(0.5 * (lo + hi), lo, hi)
        return lo, hi

    col = lambda v: jnp.full((nr, 1), v, F32)
    _, _, lo, hi = lax.fori_loop(0, TOPK_EXP_STEPS, by_exponent, (col(F32_MIN_EXP), col(1.0), col(0.0), col(2.0)))
    lo, hi = lax.fori_loop(0, TOPK_VALUE_STEPS, by_value, (lo, hi))
    gt = a >= hi
    eq = (a >= lo) & (a < hi)
    need = cap - count(gt)
    ut = ut_ref[...]
    run_eq = jnp.zeros((nr, 1), F32)
    run_sel = jnp.zeros((nr, 1), F32)
    for ch in range(seq // LANES):
        sl = slice(ch * LANES, (ch + 1) * LANES)
        eq_rank = jnp.dot(eq[:, sl].astype(BF16), ut, preferred_element_type=F32) + run_eq
        run_eq = eq_rank[:, LANES - 1:LANES]
        sel = gt[:, sl] | (eq[:, sl] & (eq_rank <= need))
        pos = jnp.dot(sel.astype(BF16), ut, preferred_element_type=F32) + run_sel
        run_sel = pos[:, LANES - 1:LANES]
        pos_ref[:, sl] = jnp.where(sel, pos.astype(I32) - 1, -1)


def _topk(aff2, cap):
    nr, seq = aff2.shape
    ut = jnp.asarray(np.triu(np.ones((LANES, LANES), np.float32))).astype(BF16)
    return pl.pallas_call(
        functools.partial(_topk_kernel, cap=cap),
        grid=(1,),
        in_specs=[pl.BlockSpec((nr, seq), lambda i: (0, 0)), pl.BlockSpec((LANES, LANES), lambda i: (0, 0))],
        out_specs=pl.BlockSpec((nr, seq), lambda i: (0, 0)),
        out_shape=jax.ShapeDtypeStruct((nr, seq), I32),
        compiler_params=_params("arbitrary"),
        name="topk",
    )(aff2, ut)


def _window_start(st_ref, idx, w, cap):
    base = st_ref[idx] + w * WINDOW
    return base, pl.multiple_of(jnp.minimum(base, cap - WINDOW), BF16_SUBLANES)


def _gather_kernel(st_ref, np_ref, pos_ref, h2_ref, xe_ref):
    eh, tile = pos_ref.shape[1:]
    cap = xe_ref.shape[2]
    nt = pl.num_programs(2)
    b, half, i = pl.program_id(0), pl.program_id(1), pl.program_id(2)
    ne = pl.num_programs(1) * eh

    @pl.when(i == 0)
    def _():
        xe_ref[...] = jnp.zeros_like(xe_ref)

    rho = lax.broadcasted_iota(I32, (WINDOW, tile), 0)

    def one_pass(w, carry):
        starts, rows = [], []
        for el in range(eh):
            base, st = _window_start(st_ref, (b * ne + half * eh + el) * nt + i, w, cap)
            pos = pos_ref[0, el:el + 1, :]
            rows.append(((pos - st == rho) & (pos >= base)).astype(BF16))
            starts.append(st)
        got = jnp.dot(jnp.concatenate(rows, axis=0), h2_ref[0], preferred_element_type=F32).astype(BF16)
        for el in range(eh):
            xe_ref[0, el, pl.ds(starts[el], WINDOW), :] += got[el * WINDOW:(el + 1) * WINDOW]
        return carry

    lax.fori_loop(0, np_ref[(b * pl.num_programs(1) + half) * nt + i], one_pass, 0)


def _gather(st16, npass_half, pos, h2, cap):
    bsz, ne, seq = pos.shape
    d = h2.shape[2]
    tile = TOKEN_TILE
    eh = ne // GATHER_SPLIT
    grid_spec = pltpu.PrefetchScalarGridSpec(
        num_scalar_prefetch=2,
        grid=(bsz, GATHER_SPLIT, seq // tile),
        in_specs=[pl.BlockSpec((1, eh, tile), lambda b, h, i, st, n: (b, h, i)),
                  pl.BlockSpec((1, tile, d), lambda b, h, i, st, n: (b, i, 0))],
        out_specs=pl.BlockSpec((1, eh, cap, d), lambda b, h, i, st, n: (b, h, 0, 0)))
    return pl.pallas_call(
        _gather_kernel,
        grid_spec=grid_spec,
        out_shape=jax.ShapeDtypeStruct((bsz, ne, cap, d), BF16),
        compiler_params=_params("parallel", "parallel", "arbitrary"),
        name="gather",
    )(st16, npass_half, pos, h2)


def _moe_kernel(xe_ref, wg_ref, wu_ref, wd_ref, ye_ref):
    tile = TOKEN_TILE
    for r in range(xe_ref.shape[2] // tile):
        xe = xe_ref[0, 0, r * tile:(r + 1) * tile, :]
        a = jnp.dot(xe, wg_ref[0], preferred_element_type=F32)
        u = jnp.dot(xe, wu_ref[0], preferred_element_type=F32)
        hmid = (a * jax.nn.sigmoid(a) * u).astype(BF16)
        ye_ref[0, 0, r * tile:(r + 1) * tile, :] = jnp.dot(hmid, wd_ref[0],
                                                          preferred_element_type=F32).astype(BF16)


def _moe(xe, wg, wu, wd):
    bsz, ne, cap, d = xe.shape
    dff = wg.shape[2]
    return pl.pallas_call(
        _moe_kernel,
        grid=(ne, bsz),
        in_specs=[pl.BlockSpec((1, 1, cap, d), lambda e, b: (b, e, 0, 0)),
                  pl.BlockSpec((1, d, dff), lambda e, b: (e, 0, 0)),
                  pl.BlockSpec((1, d, dff), lambda e, b: (e, 0, 0)),
                  pl.BlockSpec((1, dff, d), lambda e, b: (e, 0, 0))],
        out_specs=pl.BlockSpec((1, 1, cap, d), lambda e, b: (b, e, 0, 0)),
        out_shape=jax.ShapeDtypeStruct((bsz, ne, cap, d), BF16),
        compiler_params=_params("parallel", "parallel"),
        name="moe",
    )(xe, wg, wu, wd)


def _combine_kernel(st_ref, np_ref, pos_ref, aff_ref, ye_ref, x1_ref, mod_ref, ex_ref, rho_ref, o_ref,
                    rhs_ref, acc_ref):
    tile, ne = pos_ref.shape[1:]
    cap = ye_ref.shape[2]
    nt = pl.num_programs(1)
    b, i = pl.program_id(0), pl.program_id(1)
    acc_ref[...] = jnp.zeros_like(acc_ref)
    lane = lax.broadcasted_iota(I32, (1, ne), 1)
    pos = pos_ref[0]
    aexp = jnp.dot(aff_ref[0].astype(BF16), ex_ref[...], preferred_element_type=F32)

    def one_pass(w, carry):
        base_v = jnp.zeros((1, ne), I32)
        st_v = jnp.zeros((1, ne), I32)
        for e in range(ne):
            base, st = _window_start(st_ref, (b * ne + e) * nt + i, w, cap)
            rhs_ref[e * WINDOW:(e + 1) * WINDOW, :] = ye_ref[0, e, pl.ds(st, WINDOW), :]
            base_v = jnp.where(lane == e, base, base_v)
            st_v = jnp.where(lane == e, st, st_v)
        off = pos - st_v
        off = jnp.where((pos >= base_v) & (off >= 0) & (off < WINDOW), off, -1)
        oexp = jnp.dot(off.astype(F32).astype(BF16), ex_ref[...], preferred_element_type=F32)
        pt = jnp.where(oexp == rho_ref[...], aexp, 0.0).astype(BF16)
        acc_ref[...] += jnp.dot(pt, rhs_ref[...], preferred_element_type=F32)
        return carry

    lax.fori_loop(0, np_ref[b * nt + i], one_pass, 0)
    o_ref[0] = x1_ref[0] + mod_ref[0, 5:6, :] * acc_ref[...]


def _combine(st16, npass, pos_tok, aff_tok, ye, x1, mod3):
    bsz, seq, d = x1.shape
    ne, cap = ye.shape[1:3]
    tile = TOKEN_TILE
    spread = np.kron(np.eye(ne, dtype=np.float32), np.ones((1, WINDOW), np.float32))
    rho = np.tile(np.arange(WINDOW, dtype=np.float32), ne).reshape(1, ne * WINDOW)
    grid_spec = pltpu.PrefetchScalarGridSpec(
        num_scalar_prefetch=2,
        grid=(bsz, seq // tile),
        in_specs=[pl.BlockSpec((1, tile, ne), lambda b, i, st, n: (b, i, 0)),
                  pl.BlockSpec((1, tile, ne), lambda b, i, st, n: (b, i, 0)),
                  pl.BlockSpec((1, ne, cap, d), lambda b, i, st, n: (b, 0, 0, 0), pipeline_mode=pl.Buffered(1)),
                  pl.BlockSpec((1, tile, d), lambda b, i, st, n: (b, i, 0)),
                  pl.BlockSpec((1, 6, d), lambda b, i, st, n: (b, 0, 0)),
                  pl.BlockSpec((ne, ne * WINDOW), lambda b, i, st, n: (0, 0)),
                  pl.BlockSpec((1, ne * WINDOW), lambda b, i, st, n: (0, 0))],
        out_specs=pl.BlockSpec((1, tile, d), lambda b, i, st, n: (b, i, 0)),
        scratch_shapes=[pltpu.VMEM((ne * WINDOW, d), BF16), pltpu.VMEM((tile, d), F32)])
    return pl.pallas_call(
        _combine_kernel,
        grid_spec=grid_spec,
        out_shape=jax.ShapeDtypeStruct((bsz, seq, d), F32),
        compiler_params=_params("parallel", "arbitrary"),
        name="combine",
    )(st16, npass, pos_tok, aff_tok, ye, x1, mod3, jnp.asarray(spread).astype(BF16), jnp.asarray(rho))


def _route_tables(pos):
    bsz, ne, seq = pos.shape
    tile = TOKEN_TILE
    counts = jnp.sum((pos >= 0).reshape(bsz, ne, seq // tile, tile), axis=-1, dtype=I32)
    starts = jnp.cumsum(counts, axis=-1) - counts
    st16 = (starts // BF16_SUBLANES) * BF16_SUBLANES
    span = jnp.where(counts > 0, starts - st16 + counts, 0)
    passes = (span + WINDOW - 1) // WINDOW
    npass = jnp.max(passes, axis=1)
    npass_half = jnp.max(passes.reshape(bsz, GATHER_SPLIT, ne // GATHER_SPLIT, -1), axis=2)
    return st16.reshape(-1), npass.reshape(-1), npass_half.reshape(-1)


def kernel(x, c, w_ada, b_ada, g_mix, g_ffn, w_in, b_in, short_w, short_b, hy_w1, hy_b1, hy_w2, hy_b2, hy_w3, hy_b3, hy_w4, hy_freq, hy_bias, q_gain, k_gain, w_hy_out, w_att_out, w_out, w_router, w_gate, w_up, w_down):
    bsz, seq, d = x.shape
    depth = w_ada.shape[0]
    ne = w_router.shape[-1]
    cap = EC_FACTOR * seq // ne
    for l in range(depth):
        mod3 = _adaln(c, w_ada[l], b_ada[l]).reshape(bsz, 6, d)
        zhyT, q, k, v, gates = _inproj(x, mod3, g_mix[l], w_in[l], b_in[l], q_gain[l], k_gain[l])
        hT = _filters(seq, hy_w1[l], hy_b1[l], hy_w2[l], hy_b2[l], hy_w3[l], hy_b3[l], hy_w4[l], hy_freq[l])
        yhyT = _hyena(zhyT, hT, short_w[l], short_b[l], hy_bias[l])
        yat = _attention(q, k, v, q_gain[l], k_gain[l])
        x1, h2, aff = _merge(x, yhyT, yat, gates, mod3, g_ffn[l], w_hy_out[l], w_att_out[l], w_out[l],
                             w_router[l])
        pos = _topk(aff.transpose(0, 2, 1).reshape(bsz * ne, seq), cap).reshape(bsz, ne, seq)
        st16, npass, npass_half = _route_tables(pos)
        xe = _gather(st16, npass_half, pos, h2, cap)
        ye = _moe(xe, w_gate[l].astype(BF16), w_up[l].astype(BF16), w_down[l].astype(BF16))
        x = _combine(st16, npass, pos.transpose(0, 2, 1), aff, ye, x1, mod3)
    return x
```

```python
import functools
import math

import numpy as np
import jax
import jax.numpy as jnp
from jax import lax
from jax.experimental import pallas as pl
from jax.experimental.pallas import tpu as pltpu

F32 = jnp.float32
BF16 = jnp.bfloat16
I32 = jnp.int32
HIGHEST = lax.Precision.HIGHEST

EPS = 1e-6
GRID_W = 64
HY_WIDTH = 512
HY_ORDER = 2
SHORT_K = 3
FILTER_EMB = 33
FILTER_HIDDEN = 64
HY_QUICK_DECAY_PCT = 0.3
HY_GRADUAL_DECAY_PCT = 1.5
HY_DECAY_TARGET = 1e-2
N_HEADS = 8
N_KV_HEADS = 2
HEAD_DIM = 64
ROPE_THETA = 10000.0
N_EXPERTS = 16
EC_FACTOR = 2
HY_IN = 3 * HY_WIDTH
ATT_WIDTH = N_HEADS * HEAD_DIM
KV_WIDTH = N_KV_HEADS * HEAD_DIM

LANES = 128
VMEM_LIMIT = 56 * 1024 * 1024
MXU_TILE = 256
TOKEN_TILE = 256
Q_TILE = 256
KV_CHUNK = 1024
HY_CH_BLOCK = 8
HY_CH_BATCH = 4
FILTER_ROWS = 128
WINDOW = 64
BF16_SUBLANES = 16
GATHER_SPLIT = 2
SAFE_SCORE_BOUND = 48.0
BF16_NORM_SLACK = 1.01
F32_MIN_EXP = -150.0
TOPK_EXP_STEPS = 10
TOPK_VALUE_STEPS = 28
ONES_ROWS = 16
Q_SCALE = HEAD_DIM ** -0.5 * math.log2(math.e)

NT_DIMS = (((1,), (1,)), ((), ()))


def _params(*sem):
    return pltpu.CompilerParams(dimension_semantics=sem, vmem_limit_bytes=VMEM_LIMIT)


def _split2(a):
    hi = a.astype(BF16)
    return hi, (a - hi.astype(F32)).astype(BF16)


def _dot_split_lhs(a, b_bf16):
    hi, lo = _split2(a)
    return (jnp.dot(hi, b_bf16, preferred_element_type=F32)
            + jnp.dot(lo, b_bf16, preferred_element_type=F32))


def _adaln_kernel(c_ref, w_ref, b_ref, o_ref):
    c = c_ref[...]
    s = c * jax.nn.sigmoid(c)
    o_ref[...] = jnp.dot(s, w_ref[...], precision=HIGHEST, preferred_element_type=F32) + b_ref[...]


def _adaln(c, w, b):
    bsz, d = c.shape
    n = w.shape[1]
    rows = 8
    cp = jnp.zeros((rows, d), F32).at[:bsz].set(c)
    tn = 1536
    out = pl.pallas_call(
        _adaln_kernel,
        grid=(n // tn,),
        in_specs=[pl.BlockSpec((rows, d), lambda j: (0, 0)),
                  pl.BlockSpec((d, tn), lambda j: (0, j)),
                  pl.BlockSpec((1, tn), lambda j: (0, j))],
        out_specs=pl.BlockSpec((rows, tn), lambda j: (0, j)),
        out_shape=jax.ShapeDtypeStruct((rows, n), F32),
        compiler_params=_params("parallel"),
        name="adaln",
    )(cp, w, b.reshape(1, n))
    return out[:bsz]


def _inproj_kernel(x_ref, mod_ref, g_ref, whyT_ref, bhy_ref, wr_ref, br_ref, cos_ref, sin_ref,
                   qg_ref, kg_ref, bd_ref, zhyT_ref, q_ref, k_ref, vt_ref, gate_ref):
    x = x_ref[0]
    ms = jnp.mean(x * x, axis=-1, keepdims=True)
    sh1 = mod_ref[0, 0:1, :]
    sc1 = mod_ref[0, 1:2, :]
    h = (x * lax.rsqrt(ms + EPS)) * g_ref[...] * (1.0 + sc1) + sh1
    hb = h.astype(BF16)
    zhyT_ref[0] = lax.dot_general(whyT_ref[...], hb, NT_DIMS, preferred_element_type=F32) + bhy_ref[...]
    zr = jnp.dot(hb, wr_ref[...], preferred_element_type=F32) + br_ref[...]

    cos = cos_ref[...]
    sin = sin_ref[...]
    bd = bd_ref[...]
    lane = lax.broadcasted_iota(I32, cos.shape, 1)
    first = (lane & (HEAD_DIM // 2 - 1)) < HEAD_DIM // 4

    def norm_rope(u, gain, scale):
        ss = _dot_split_lhs(u * u, bd)
        un = (u * lax.rsqrt(ss * (1.0 / HEAD_DIM) + EPS)) * gain
        rot = jnp.where(first, pltpu.roll(un, LANES - HEAD_DIM // 4, 1), pltpu.roll(un, HEAD_DIM // 4, 1))
        return (un * cos + rot * sin) * scale

    per_group = LANES // HEAD_DIM
    for j in range(ATT_WIDTH // LANES):
        qn = norm_rope(zr[:, j * LANES:(j + 1) * LANES], qg_ref[...], Q_SCALE).astype(BF16)
        for hl in range(per_group):
            q_ref[0, j * per_group + hl] = qn[:, hl * HEAD_DIM:(hl + 1) * HEAD_DIM]
    kn = norm_rope(zr[:, ATT_WIDTH:ATT_WIDTH + KV_WIDTH], kg_ref[...], 1.0).astype(BF16)
    for hl in range(N_KV_HEADS):
        k_ref[0, hl] = kn[:, hl * HEAD_DIM:(hl + 1) * HEAD_DIM]
    vt = zr[:, ATT_WIDTH + KV_WIDTH:ATT_WIDTH + 2 * KV_WIDTH].T.astype(BF16)
    for hl in range(N_KV_HEADS):
        vt_ref[0, hl, :HEAD_DIM, :] = vt[hl * HEAD_DIM:(hl + 1) * HEAD_DIM]
        vt_ref[0, hl, HEAD_DIM:, :] = jnp.ones((ONES_ROWS, vt.shape[1]), BF16)
    gate_ref[0] = jax.nn.sigmoid(zr[:, ATT_WIDTH + 2 * KV_WIDTH:])


def _rope_tables(seq):
    rows = seq // GRID_W
    t = np.arange(seq)
    row = (t // GRID_W).astype(np.float32)
    col = (t % GRID_W).astype(np.float32)
    half = HEAD_DIM // 2
    quarter = half // 2
    inv = (ROPE_THETA ** (-np.arange(0, half, 2, dtype=np.float32) / half)).astype(np.float32)
    ang_r = row[:, None] * inv[None, :]
    ang_c = col[:, None] * inv[None, :]
    ang = np.concatenate([ang_r, ang_r, ang_c, ang_c], axis=1)
    sign = np.concatenate([-np.ones(quarter), np.ones(quarter)] * 2).astype(np.float32)
    cos = np.cos(ang).astype(np.float32)
    sin = (np.sin(ang) * sign[None, :]).astype(np.float32)
    reps = LANES // HEAD_DIM
    del rows
    return jnp.asarray(np.tile(cos, (1, reps))), jnp.asarray(np.tile(sin, (1, reps)))


def _inproj(x, mod3, g_mix, w_in, b_in, q_gain, k_gain):
    bsz, seq, d = x.shape
    tm = TOKEN_TILE
    n_rest = w_in.shape[1] - HY_IN
    n_gate = n_rest - ATT_WIDTH - 2 * KV_WIDTH
    whyT = w_in[:, :HY_IN].T.astype(BF16)
    wr = w_in[:, HY_IN:].astype(BF16)
    bhy = b_in[:HY_IN].reshape(HY_IN, 1)
    br = b_in[HY_IN:].reshape(1, n_rest)
    cos, sin = _rope_tables(seq)
    reps = LANES // HEAD_DIM
    qg = jnp.tile(q_gain, reps).reshape(1, LANES)
    kg = jnp.tile(k_gain, reps).reshape(1, LANES)
    bd = jnp.asarray(np.kron(np.eye(reps, dtype=np.float32), np.ones((HEAD_DIM, HEAD_DIM), np.float32))).astype(BF16)
    full = lambda shape: pl.BlockSpec(shape, lambda b, i: (0,) * len(shape))
    return pl.pallas_call(
        _inproj_kernel,
        grid=(bsz, seq // tm),
        in_specs=[pl.BlockSpec((1, tm, d), lambda b, i: (b, i, 0)),
                  pl.BlockSpec((1, 6, d), lambda b, i: (b, 0, 0)),
                  full((1, d)), full((HY_IN, d)), full((HY_IN, 1)), full((d, n_rest)), full((1, n_rest)),
                  pl.BlockSpec((tm, LANES), lambda b, i: (i, 0)),
                  pl.BlockSpec((tm, LANES), lambda b, i: (i, 0)),
                  full((1, LANES)), full((1, LANES)), full((LANES, LANES))],
        out_specs=[pl.BlockSpec((1, HY_IN, tm), lambda b, i: (b, 0, i)),
                   pl.BlockSpec((1, N_HEADS, tm, HEAD_DIM), lambda b, i: (b, 0, i, 0)),
                   pl.BlockSpec((1, N_KV_HEADS, tm, HEAD_DIM), lambda b, i: (b, 0, i, 0)),
                   pl.BlockSpec((1, N_KV_HEADS, HEAD_DIM + ONES_ROWS, tm), lambda b, i: (b, 0, 0, i)),
                   pl.BlockSpec((1, tm, n_gate), lambda b, i: (b, i, 0))],
        out_shape=[jax.ShapeDtypeStruct((bsz, HY_IN, seq), F32),
                   jax.ShapeDtypeStruct((bsz, N_HEADS, seq, HEAD_DIM), BF16),
                   jax.ShapeDtypeStruct((bsz, N_KV_HEADS, seq, HEAD_DIM), BF16),
                   jax.ShapeDtypeStruct((bsz, N_KV_HEADS, HEAD_DIM + ONES_ROWS, seq), BF16),
                   jax.ShapeDtypeStruct((bsz, seq, n_gate), F32)],
        compiler_params=_params("parallel", "parallel"),
        name="inproj",
    )(x, mod3, g_mix.reshape(1, d), whyT, bhy, wr, br, cos, sin, qg, kg, bd)


def _filter_kernel(zT_ref, w1T_ref, b1_ref, w2T_ref, b2_ref, w3T_ref, b3_ref, fr_ref, w4T_ref,
                   t01_ref, absd_ref, o_ref, hdn_ref):
    @pl.when(pl.program_id(0) == 0)
    def _():
        fr = fr_ref[...]
        h = jnp.sin(fr[:, 0:1] * (jnp.dot(w1T_ref[...], zT_ref[...], precision=HIGHEST,
                                          preferred_element_type=F32) + b1_ref[...]))
        h = jnp.sin(fr[:, 1:2] * (jnp.dot(w2T_ref[...], h, precision=HIGHEST,
                                          preferred_element_type=F32) + b2_ref[...]))
        h = jnp.sin(fr[:, 2:3] * (jnp.dot(w3T_ref[...], h, precision=HIGHEST,
                                          preferred_element_type=F32) + b3_ref[...]))
        hdn_ref[...] = h

    w_hi, w_lo = _split2(w4T_ref[...])
    h_hi, h_lo = _split2(hdn_ref[...])
    hT = (jnp.dot(w_hi, h_hi, preferred_element_type=F32) + jnp.dot(w_lo, h_hi, preferred_element_type=F32)
          + jnp.dot(w_hi, h_lo, preferred_element_type=F32))
    hT = hT * jnp.exp(-t01_ref[...] * absd_ref[...])
    nrm = jnp.sum(jnp.abs(hT), axis=1, keepdims=True) + EPS
    o_ref[...] = hT * (1.0 / nrm)


def _filters(seq, w1, b1, w2, b2, w3, b3, w4, freq):
    emb_pad = 64
    bands = (FILTER_EMB - 1) // 2
    t01 = np.linspace(0.0, 1.0, seq, dtype=np.float32)[None, :]
    f = np.linspace(1e-4, bands - 1, bands, dtype=np.float32)[:, None]
    w = ((2.0 * math.pi) * np.arange(seq, dtype=np.float32) / seq).astype(np.float32)[None, :]
    zT = np.zeros((emb_pad, seq), np.float32)
    zT[0:1] = t01
    zT[1:1 + bands] = np.cos(f * w)
    zT[1 + bands:1 + 2 * bands] = -np.sin(f * w)
    w1T = jnp.zeros((FILTER_HIDDEN, emb_pad), F32).at[:, :FILTER_EMB].set(w1.T)
    max_decay = math.log(HY_DECAY_TARGET) / HY_QUICK_DECAY_PCT
    min_decay = math.log(HY_DECAY_TARGET) / HY_GRADUAL_DECAY_PCT
    deltas = np.abs(np.linspace(min_decay, max_decay, HY_WIDTH, dtype=np.float32))
    n_rows = HY_ORDER * 2 * HY_WIDTH
    absd = np.tile(deltas, HY_ORDER * 2).reshape(n_rows, 1)
    rb = FILTER_ROWS
    full = lambda shape: pl.BlockSpec(shape, lambda j: (0,) * len(shape))
    col = lambda v: v.reshape(FILTER_HIDDEN, 1)
    return pl.pallas_call(
        _filter_kernel,
        grid=(n_rows // rb,),
        in_specs=[full((emb_pad, seq)),
                  full((FILTER_HIDDEN, emb_pad)), full((FILTER_HIDDEN, 1)),
                  full((FILTER_HIDDEN, FILTER_HIDDEN)), full((FILTER_HIDDEN, 1)),
                  full((FILTER_HIDDEN, FILTER_HIDDEN)), full((FILTER_HIDDEN, 1)),
                  full((FILTER_HIDDEN, 3)),
                  pl.BlockSpec((rb, FILTER_HIDDEN), lambda j: (j, 0)),
                  full((1, seq)),
                  pl.BlockSpec((rb, 1), lambda j: (j, 0))],
        out_specs=pl.BlockSpec((rb, seq), lambda j: (j, 0)),
        out_shape=jax.ShapeDtypeStruct((n_rows, seq), F32),
        scratch_shapes=[pltpu.VMEM((FILTER_HIDDEN, seq), F32)],
        compiler_params=_params("arbitrary"),
        name="filters",
    )(jnp.asarray(zT), w1T, col(b1), w2.T, col(b2), w3.T, col(b3), freq.T, w4.T,
      jnp.asarray(t01), jnp.asarray(absd))


def _dft_tables(seq):
    n = 2 * seq
    r = int(round(math.sqrt(n)))
    assert r * r == n, "2*seq must be a perfect square"
    hh = r // 2
    rh = -(-(hh + 1) // BF16_SUBLANES) * BF16_SUBLANES
    k = np.arange(r, dtype=np.float64)
    ang = 2.0 * np.pi * np.outer(k, k) / r
    fr, fi = np.cos(ang), -np.sin(ang)
    angt = 2.0 * np.pi * np.outer(k, k) / n
    keep = np.zeros((rh, 1))
    keep[:hh + 1] = 1.0
    pad = lambda a: np.concatenate([a[:hh + 1], np.zeros((rh - hh - 1,) + a.shape[1:])], axis=0)
    twr, twi = pad(np.cos(angt)), pad(-np.sin(angt))
    f1 = np.concatenate([pad(fr[:, :hh]), pad(fi[:, :hh])], axis=0)
    f3 = np.block([[fr, fi], [-fi, fr]])
    f3i = np.block([[fr, -fi], [fi, fr]])
    weight = np.full((rh, 1), 2.0) * keep
    weight[0] = weight[hh] = 1.0
    f1i = np.concatenate([(pad(fr[:, :hh]) * weight).T, (pad(fi[:, :hh]) * weight).T], axis=1) / n
    as32 = lambda a: jnp.asarray(a.astype(np.float32))
    return as32(f1), as32(twr), as32(twi), as32(f3), as32(f3i), as32(f1i)


def _hyena_kernel(sw_ref, sb_ref, fb_ref, v_ref, x1_ref, x2_ref, h_ref, f1_f32, twr_ref, twi_ref,
                  f3_f32, f3i_f32, f1i_f32, o_ref, f1_ref, f3_ref, f3i_ref, f1i_ref):
    bsz, cb, hh, r = v_ref.shape
    rh = twr_ref.shape[0]
    j = pl.program_id(0)
    f1_ref[...] = f1_f32[...].astype(BF16)
    f3_ref[...] = f3_f32[...].astype(BF16)
    f3i_ref[...] = f3i_f32[...].astype(BF16)
    f1i_ref[...] = f1i_f32[...].astype(BF16)
    twr = twr_ref[...]
    twi = twi_ref[...]
    lane = lax.broadcasted_iota(I32, (hh, r), 1)
    row = lax.broadcasted_iota(I32, (hh, r), 0)

    def shift_prev(z):
        a = pltpu.roll(z, 1, 1)
        a = jnp.where(lane == 0, pltpu.roll(a, 1, 0), a)
        return jnp.where((lane == 0) & (row == 0), 0.0, a)

    def shift_next(z):
        a = pltpu.roll(z, r - 1, 1)
        a = jnp.where(lane == r - 1, pltpu.roll(a, hh - 1, 0), a)
        return jnp.where((lane == r - 1) & (row == hh - 1), 0.0, a)

    def fwd(slabs):
        m = len(slabs)
        acat = jnp.concatenate([a.astype(BF16) for a in slabs], axis=1)
        s = jnp.dot(f1_ref[...], acat, preferred_element_type=F32)
        lhs = []
        for i in range(m):
            br = s[:rh, i * r:(i + 1) * r]
            bi = s[rh:, i * r:(i + 1) * r]
            cr = br * twr - bi * twi
            ci = br * twi + bi * twr
            lhs.append(jnp.concatenate([cr, ci], axis=1).astype(BF16))
        d = jnp.dot(jnp.concatenate(lhs, axis=0), f3_ref[...], preferred_element_type=F32)
        return [(d[i * rh:(i + 1) * rh, :r], d[i * rh:(i + 1) * rh, r:]) for i in range(m)]

    def inv(specs):
        m = len(specs)
        lhs = jnp.concatenate([jnp.concatenate([pr, pi], axis=1).astype(BF16) for pr, pi in specs], axis=0)
        e = jnp.dot(lhs, f3i_ref[...], preferred_element_type=F32)
        rhs = []
        for i in range(m):
            er = e[i * rh:(i + 1) * rh, :r]
            ei = e[i * rh:(i + 1) * rh, r:]
            tr = er * twr + ei * twi
            ti = ei * twr - er * twi
            rhs.append(jnp.concatenate([tr, ti], axis=0).astype(BF16))
        y = jnp.dot(f1i_ref[...], jnp.concatenate(rhs, axis=1), preferred_element_type=F32)
        return [y[:, i * r:(i + 1) * r] for i in range(m)]

    def cmul(a, g):
        return a[0] * g[0] - a[1] * g[1], a[0] * g[1] + a[1] * g[0]

    def sconv(ref, b, c, gi):
        z = ref[b, c]
        return (sb_ref[gi] + sw_ref[gi] * shift_prev(z) + sw_ref[HY_IN + gi] * z
                + sw_ref[2 * HY_IN + gi] * shift_next(z))

    nb = HY_CH_BATCH

    def body(t, carry):
        cs = [t * nb + i for i in range(nb)]
        chs = [j * cb + c for c in cs]
        pairs = [(c, ch, b) for c, ch in zip(cs, chs) for b in range(bsz)]
        v = [sconv(v_ref, b, c, ch) for c, ch, b in pairs]
        filt = [h_ref[o, d, c] for c in cs for o in range(HY_ORDER) for d in range(2)]
        spec = fwd(v + filt)
        nv = len(v)
        g = []
        for i in range(nb):
            gi = []
            for o in range(HY_ORDER):
                hf = spec[nv + (i * HY_ORDER + o) * 2]
                hb = spec[nv + (i * HY_ORDER + o) * 2 + 1]
                gi.append((hf[0] + hb[0], hf[1] - hb[1]))
            g.append(gi)
        conv = inv([cmul(spec[n], g[n // bsz][0]) for n in range(nv)])
        y = [sconv(x1_ref, b, c, HY_WIDTH + ch) * (conv[n] + fb_ref[ch] * v[n])
             for n, (c, ch, b) in enumerate(pairs)]
        spec = fwd(y)
        conv = inv([cmul(spec[n], g[n // bsz][1]) for n in range(nv)])
        for n, (c, ch, b) in enumerate(pairs):
            o_ref[b, c] = (sconv(x2_ref, b, c, 2 * HY_WIDTH + ch)
                           * (conv[n] + fb_ref[HY_WIDTH + ch] * y[n]))
        return carry

    lax.fori_loop(0, cb // nb, body, 0)


def _hyena(zhyT, hT, short_w, short_b, filt_bias):
    bsz, _, seq = zhyT.shape
    tabs = _dft_tables(seq)
    r = tabs[1].shape[1]
    hh = r // 2
    z4 = zhyT.reshape(bsz, HY_IN, hh, r)
    h5 = hT.reshape(HY_ORDER, 2, HY_WIDTH, hh, r)
    cb = HY_CH_BLOCK
    nblk = HY_WIDTH // cb
    smem = pl.BlockSpec(memory_space=pltpu.SMEM)
    full = lambda a: pl.BlockSpec(a.shape, lambda j: (0,) * a.ndim)
    zspec = lambda off: pl.BlockSpec((bsz, cb, hh, r), lambda j: (0, off * nblk + j, 0, 0))
    out = pl.pallas_call(
        _hyena_kernel,
        grid=(nblk,),
        in_specs=[smem, smem, smem, zspec(0), zspec(1), zspec(2),
                  pl.BlockSpec((HY_ORDER, 2, cb, hh, r), lambda j: (0, 0, j, 0, 0))]
                 + [full(t) for t in tabs],
        out_specs=pl.BlockSpec((bsz, cb, hh, r), lambda j: (0, j, 0, 0)),
        out_shape=jax.ShapeDtypeStruct((bsz, HY_WIDTH, hh, r), F32),
        scratch_shapes=[pltpu.VMEM(tabs[i].shape, BF16) for i in (0, 3, 4, 5)],
        compiler_params=_params("parallel"),
        name="hyena",
    )(short_w.reshape(-1), short_b, filt_bias.reshape(-1), z4, z4, z4, h5, *tabs)
    return out.reshape(bsz, HY_WIDTH, seq)


def _attn_kernel(bounded_ref, q_ref, k_ref, vt_ref, o_ref):
    g, tq, hd = q_ref.shape[1:]
    seq = k_ref.shape[2]
    rows = vt_ref.shape[2]
    n_chunks = seq // KV_CHUNK
    q = q_ref[0].reshape(g * tq, hd)

    def scores(kc):
        kb = k_ref[0, 0, kc * KV_CHUNK:(kc + 1) * KV_CHUNK, :]
        return lax.dot_general(kb, q, NT_DIMS, preferred_element_type=F32)

    def values(kc):
        return vt_ref[0, 0, :, kc * KV_CHUNK:(kc + 1) * KV_CHUNK]

    def finish(acc):
        out = acc[:hd] / acc[hd:hd + 1]
        o_ref[0] = out.T.reshape(g, tq, hd).astype(o_ref.dtype)

    @pl.when(bounded_ref[0] == 1)
    def _():
        acc = jnp.zeros((rows, g * tq), F32)
        for kc in range(n_chunks):
            p = jnp.exp2(scores(kc)).astype(BF16)
            acc = acc + jnp.dot(values(kc), p, preferred_element_type=F32)
        finish(acc)

    @pl.when(bounded_ref[0] != 1)
    def _():
        m = jnp.full((1, g * tq), -jnp.inf, F32)
        acc = jnp.zeros((rows, g * tq), F32)
        s_next = scores(0)
        for kc in range(n_chunks):
            s = s_next
            if kc + 1 < n_chunks:
                s_next = scores(kc + 1)
            m_new = jnp.maximum(m, jnp.max(s, axis=0, keepdims=True))
            p = jnp.exp2(s - m_new).astype(BF16)
            acc = jnp.exp2(m - m_new) * acc + jnp.dot(values(kc), p, preferred_element_type=F32)
            m = m_new
        finish(acc)


def _attention(qh, kh, vt, q_gain, k_gain):
    bsz, _, seq, _ = qh.shape
    g = N_HEADS // N_KV_HEADS
    rows = vt.shape[2]
    bound = HEAD_DIM * Q_SCALE * BF16_NORM_SLACK * jnp.max(jnp.abs(q_gain)) * jnp.max(jnp.abs(k_gain))
    bounded = (bound <= SAFE_SCORE_BOUND).astype(I32).reshape(1)
    tq = Q_TILE
    grid_spec = pltpu.PrefetchScalarGridSpec(
        num_scalar_prefetch=1,
        grid=(bsz, N_KV_HEADS, seq // tq),
        in_specs=[pl.BlockSpec((1, g, tq, HEAD_DIM), lambda b, h, i, f: (b, h, i, 0)),
                  pl.BlockSpec((1, 1, seq, HEAD_DIM), lambda b, h, i, f: (b, h, 0, 0)),
                  pl.BlockSpec((1, 1, rows, seq), lambda b, h, i, f: (b, h, 0, 0))],
        out_specs=pl.BlockSpec((1, g, tq, HEAD_DIM), lambda b, h, i, f: (b, h, i, 0)))
    return pl.pallas_call(
        _attn_kernel,
        grid_spec=grid_spec,
        out_shape=jax.ShapeDtypeStruct((bsz, N_HEADS, seq, HEAD_DIM), BF16),
        compiler_params=_params("parallel", "parallel", "parallel"),
        name="attention",
    )(bounded, qh, kh, vt)


def _merge_kernel(x_ref, yhyT_ref, yat_ref, gate_ref, mod_ref, g_ref, why_ref, wat_ref, wout_ref, wr_ref,
                  x1_ref, h2_ref, aff_ref):
    d = x_ref.shape[2]
    yhy = yhyT_ref[0].T.astype(BF16)
    a = jnp.dot(yhy, why_ref[...], preferred_element_type=F32)
    yat = jnp.concatenate([yat_ref[0, h] for h in range(yat_ref.shape[1])], axis=1)
    b = jnp.dot(yat, wat_ref[...], preferred_element_type=F32)
    gate = gate_ref[0]
    mrg = gate[:, :d] * a + gate[:, d:] * b
    o = jnp.dot(mrg.astype(BF16), wout_ref[...], preferred_element_type=F32)
    gt1 = mod_ref[0, 2:3, :]
    sh2 = mod_ref[0, 3:4, :]
    sc2 = mod_ref[0, 4:5, :]
    x1 = x_ref[0] + gt1 * o
    x1_ref[0] = x1
    ms = jnp.mean(x1 * x1, axis=-1, keepdims=True)
    h2 = (x1 * lax.rsqrt(ms + EPS)) * g_ref[...] * (1.0 + sc2) + sh2
    hi, lo = _split2(h2)
    h2_ref[0] = hi
    w_hi, w_lo = _split2(wr_ref[...])
    logits = (jnp.dot(hi, w_hi, preferred_element_type=F32)
              + jnp.dot(lo, w_hi, preferred_element_type=F32)
              + jnp.dot(hi, w_lo, preferred_element_type=F32))
    ex = jnp.exp(logits - jnp.max(logits, axis=-1, keepdims=True))
    aff_ref[0] = ex / jnp.sum(ex, axis=-1, keepdims=True)


def _merge(x, yhyT, yat, gates, mod3, g_ffn, w_hy_out, w_att_out, w_out, w_router):
    bsz, seq, d = x.shape
    tm = TOKEN_TILE
    ne = w_router.shape[1]
    full = lambda shape: pl.BlockSpec(shape, lambda b, i: (0,) * len(shape))
    return pl.pallas_call(
        _merge_kernel,
        grid=(bsz, seq // tm),
        in_specs=[pl.BlockSpec((1, tm, d), lambda b, i: (b, i, 0)),
                  pl.BlockSpec((1, HY_WIDTH, tm), lambda b, i: (b, 0, i)),
                  pl.BlockSpec((1, N_HEADS, tm, HEAD_DIM), lambda b, i: (b, 0, i, 0)),
                  pl.BlockSpec((1, tm, 2 * d), lambda b, i: (b, i, 0)),
                  pl.BlockSpec((1, 6, d), lambda b, i: (b, 0, 0)),
                  full((1, d)), full((HY_WIDTH, d)), full((ATT_WIDTH, d)), full((d, d)), full((d, ne))],
        out_specs=[pl.BlockSpec((1, tm, d), lambda b, i: (b, i, 0)),
                   pl.BlockSpec((1, tm, d), lambda b, i: (b, i, 0)),
                   pl.BlockSpec((1, tm, ne), lambda b, i: (b, i, 0))],
        out_shape=[jax.ShapeDtypeStruct((bsz, seq, d), F32),
                   jax.ShapeDtypeStruct((bsz, seq, d), BF16),
                   jax.ShapeDtypeStruct((bsz, seq, ne), F32)],
        compiler_params=_params("parallel", "parallel"),
        name="merge",
    )(x, yhyT, yat, gates, mod3, g_ffn.reshape(1, d), w_hy_out.astype(BF16), w_att_out.astype(BF16),
      w_out.astype(BF16), w_router)


def _topk_kernel(a_ref, ut_ref, pos_ref, *, cap):
    nr, seq = a_ref.shape
    a = a_ref[...]

    def count(mask):
        return jnp.sum(mask.astype(F32), axis=1, keepdims=True)

    def probe(t, lo, hi):
        ok = count(a >= t) >= cap
        return ok, jnp.where(ok, t, lo), jnp.where(ok, hi, t)

    def by_exponent(_, c):
        elo, ehi, lo, hi = c
        emid = 0.5 * (elo + ehi)
        ok, lo, hi = probe(jnp.exp2(emid), lo, hi)
        return jnp.where(ok, emid, elo), jnp.where(ok, ehi, emid), lo, hi

    def by_value(_, c):
        lo, hi = c
        _, lo, hi = probe(0.5 * (lo + hi), lo, hi)
        return lo, hi

    col = lambda v: jnp.full((nr, 1), v, F32)
    _, _, lo, hi = lax.fori_loop(0, TOPK_EXP_STEPS, by_exponent, (col(F32_MIN_EXP), col(1.0), col(0.0), col(2.0)))
    lo, hi = lax.fori_loop(0, TOPK_VALUE_STEPS, by_value, (lo, hi))
    gt = a >= hi
    eq = (a >= lo) & (a < hi)
    need = cap - count(gt)
    ut = ut_ref[...]
    run_eq = jnp.zeros((nr, 1), F32)
    run_sel = jnp.zeros((nr, 1), F32)
    for ch in range(seq // LANES):
        sl = slice(ch * LANES, (ch + 1) * LANES)
        eq_rank = jnp.dot(eq[:, sl].astype(BF16), ut, preferred_element_type=F32) + run_eq
        run_eq = eq_rank[:, LANES - 1:LANES]
        sel = gt[:, sl] | (eq[:, sl] & (eq_rank <= need))
        pos = jnp.dot(sel.astype(BF16), ut, preferred_element_type=F32) + run_sel
        run_sel = pos[:, LANES - 1:LANES]
        pos_ref[:, sl] = jnp.where(sel, pos.astype(I32) - 1, -1)


def _topk(aff2, cap):
    nr, seq = aff2.shape
    ut = jnp.asarray(np.triu(np.ones((LANES, LANES), np.float32))).astype(BF16)
    return pl.pallas_call(
        functools.partial(_topk_kernel, cap=cap),
        grid=(1,),
        in_specs=[pl.BlockSpec((nr, seq), lambda i: (0, 0)), pl.BlockSpec((LANES, LANES), lambda i: (0, 0))],
        out_specs=pl.BlockSpec((nr, seq), lambda i: (0, 0)),
        out_shape=jax.ShapeDtypeStruct((nr, seq), I32),
        compiler_params=_params("arbitrary"),
        name="topk",
    )(aff2, ut)


def _window_start(st_ref, idx, w, cap):
    base = st_ref[idx] + w * WINDOW
    return base, pl.multiple_of(jnp.minimum(base, cap - WINDOW), BF16_SUBLANES)


def _gather_kernel(st_ref, np_ref, pos_ref, h2_ref, xe_ref):
    eh, tile = pos_ref.shape[1:]
    cap = xe_ref.shape[2]
    nt = pl.num_programs(2)
    b, half, i = pl.program_id(0), pl.program_id(1), pl.program_id(2)
    ne = pl.num_programs(1) * eh

    @pl.when(i == 0)
    def _():
        xe_ref[...] = jnp.zeros_like(xe_ref)

    rho = lax.broadcasted_iota(I32, (WINDOW, tile), 0)

    def one_pass(w, carry):
        starts, rows = [], []
        for el in range(eh):
            base, st = _window_start(st_ref, (b * ne + half * eh + el) * nt + i, w, cap)
            pos = pos_ref[0, el:el + 1, :]
            rows.append(((pos - st == rho) & (pos >= base)).astype(BF16))
            starts.append(st)
        got = jnp.dot(jnp.concatenate(rows, axis=0), h2_ref[0], preferred_element_type=F32).astype(BF16)
        for el in range(eh):
            xe_ref[0, el, pl.ds(starts[el], WINDOW), :] += got[el * WINDOW:(el + 1) * WINDOW]
        return carry

    lax.fori_loop(0, np_ref[(b * pl.num_programs(1) + half) * nt + i], one_pass, 0)


def _gather(st16, npass_half, pos, h2, cap):
    bsz, ne, seq = pos.shape
    d = h2.shape[2]
    tile = TOKEN_TILE
    eh = ne // GATHER_SPLIT
    grid_spec = pltpu.PrefetchScalarGridSpec(
        num_scalar_prefetch=2,
        grid=(bsz, GATHER_SPLIT, seq // tile),
        in_specs=[pl.BlockSpec((1, eh, tile), lambda b, h, i, st, n: (b, h, i)),
                  pl.BlockSpec((1, tile, d), lambda b, h, i, st, n: (b, i, 0))],
        out_specs=pl.BlockSpec((1, eh, cap, d), lambda b, h, i, st, n: (b, h, 0, 0)))
    return pl.pallas_call(
        _gather_kernel,
        grid_spec=grid_spec,
        out_shape=jax.ShapeDtypeStruct((bsz, ne, cap, d), BF16),
        compiler_params=_params("parallel", "parallel", "arbitrary"),
        name="gather",
    )(st16, npass_half, pos, h2)


def _moe_kernel(xe_ref, wg_ref, wu_ref, wd_ref, ye_ref):
    tile = TOKEN_TILE
    for r in range(xe_ref.shape[2] // tile):
        xe = xe_ref[0, 0, r * tile:(r + 1) * tile, :]
        a = jnp.dot(xe, wg_ref[0], preferred_element_type=F32)
        u = jnp.dot(xe, wu_ref[0], preferred_element_type=F32)
        hmid = (a * jax.nn.sigmoid(a) * u).astype(BF16)
        ye_ref[0, 0, r * tile:(r + 1) * tile, :] = jnp.dot(hmid, wd_ref[0],
                                                          preferred_element_type=F32).astype(BF16)


def _moe(xe, wg, wu, wd):
    bsz, ne, cap, d = xe.shape
    dff = wg.shape[2]
    return pl.pallas_call(
        _moe_kernel,
        grid=(ne, bsz),
        in_specs=[pl.BlockSpec((1, 1, cap, d), lambda e, b: (b, e, 0, 0)),
                  pl.BlockSpec((1, d, dff), lambda e, b: (e, 0, 0)),
                  pl.BlockSpec((1, d, dff), lambda e, b: (e, 0, 0)),
                  pl.BlockSpec((1, dff, d), lambda e, b: (e, 0, 0))],
        out_specs=pl.BlockSpec((1, 1, cap, d), lambda e, b: (b, e, 0, 0)),
        out_shape=jax.ShapeDtypeStruct((bsz, ne, cap, d), BF16),
        compiler_params=_params("parallel", "parallel"),
        name="moe",
    )(xe, wg, wu, wd)


def _combine_kernel(st_ref, np_ref, pos_ref, aff_ref, ye_ref, x1_ref, mod_ref, ex_ref, rho_ref, o_ref,
                    rhs_ref, acc_ref):
    tile, ne = pos_ref.shape[1:]
    cap = ye_ref.shape[2]
    nt = pl.num_programs(1)
    b, i = pl.program_id(0), pl.program_id(1)
    acc_ref[...] = jnp.zeros_like(acc_ref)
    lane = lax.broadcasted_iota(I32, (1, ne), 1)
    pos = pos_ref[0]
    aexp = jnp.dot(aff_ref[0].astype(BF16), ex_ref[...], preferred_element_type=F32)

    def one_pass(w, carry):
        base_v = jnp.zeros((1, ne), I32)
        st_v = jnp.zeros((1, ne), I32)
        for e in range(ne):
            base, st = _window_start(st_ref, (b * ne + e) * nt + i, w, cap)
            rhs_ref[e * WINDOW:(e + 1) * WINDOW, :] = ye_ref[0, e, pl.ds(st, WINDOW), :]
            base_v = jnp.where(lane == e, base, base_v)
            st_v = jnp.where(lane == e, st, st_v)
        off = pos - st_v
        off = jnp.where((pos >= base_v) & (off >= 0) & (off < WINDOW), off, -1)
        oexp = jnp.dot(off.astype(F32).astype(BF16), ex_ref[...], preferred_element_type=F32)
        pt = jnp.where(oexp == rho_ref[...], aexp, 0.0).astype(BF16)
        acc_ref[...] += jnp.dot(pt, rhs_ref[...], preferred_element_type=F32)
        return carry

    lax.fori_loop(0, np_ref[b * nt + i], one_pass, 0)
    o_ref[0] = x1_ref[0] + mod_ref[0, 5:6, :] * acc_ref[...]


def _combine(st16, npass, pos_tok, aff_tok, ye, x1, mod3):
    bsz, seq, d = x1.shape
    ne, cap = ye.shape[1:3]
    tile = TOKEN_TILE
    spread = np.kron(np.eye(ne, dtype=np.float32), np.ones((1, WINDOW), np.float32))
    rho = np.tile(np.arange(WINDOW, dtype=np.float32), ne).reshape(1, ne * WINDOW)
    grid_spec = pltpu.PrefetchScalarGridSpec(
        num_scalar_prefetch=2,
        grid=(bsz, seq // tile),
        in_specs=[pl.BlockSpec((1, tile, ne), lambda b, i, st, n: (b, i, 0)),
                  pl.BlockSpec((1, tile, ne), lambda b, i, st, n: (b, i, 0)),
                  pl.BlockSpec((1, ne, cap, d), lambda b, i, st, n: (b, 0, 0, 0), pipeline_mode=pl.Buffered(1)),
                  pl.BlockSpec((1, tile, d), lambda b, i, st, n: (b, i, 0)),
                  pl.BlockSpec((1, 6, d), lambda b, i, st, n: (b, 0, 0)),
                  pl.BlockSpec((ne, ne * WINDOW), lambda b, i, st, n: (0, 0)),
                  pl.BlockSpec((1, ne * WINDOW), lambda b, i, st, n: (0, 0))],
        out_specs=pl.BlockSpec((1, tile, d), lambda b, i, st, n: (b, i, 0)),
        scratch_shapes=[pltpu.VMEM((ne * WINDOW, d), BF16), pltpu.VMEM((tile, d), F32)])
    return pl.pallas_call(
        _combine_kernel,
        grid_spec=grid_spec,
        out_shape=jax.ShapeDtypeStruct((bsz, seq, d), F32),
        compiler_params=_params("parallel", "arbitrary"),
        name="combine",
    )(st16, npass, pos_tok, aff_tok, ye, x1, mod3, jnp.asarray(spread).astype(BF16), jnp.asarray(rho))


def _route_tables(pos):
    bsz, ne, seq = pos.shape
    tile = TOKEN_TILE
    counts = jnp.sum((pos >= 0).reshape(bsz, ne, seq // tile, tile), axis=-1, dtype=I32)
    starts = jnp.cumsum(counts, axis=-1) - counts
    st16 = (starts // BF16_SUBLANES) * BF16_SUBLANES
    span = jnp.where(counts > 0, starts - st16 + counts, 0)
    passes = (span + WINDOW - 1) // WINDOW
    npass = jnp.max(passes, axis=1)
    npass_half = jnp.max(passes.reshape(bsz, GATHER_SPLIT, ne // GATHER_SPLIT, -1), axis=2)
    return st16.reshape(-1), npass.reshape(-1), npass_half.reshape(-1)


def kernel(x, c, w_ada, b_ada, g_mix, g_ffn, w_in, b_in, short_w, short_b, hy_w1, hy_b1, hy_w2, hy_b2, hy_w3, hy_b3, hy_w4, hy_freq, hy_bias, q_gain, k_gain, w_hy_out, w_att_out, w_out, w_router, w_gate, w_up, w_down):
    bsz, seq, d = x.shape
    depth = w_ada.shape[0]
    ne = w_router.shape[-1]
    cap = EC_FACTOR * seq // ne
    for l in range(depth):
        mod3 = _adaln(c, w_ada[l], b_ada[l]).reshape(bsz, 6, d)
        zhyT, q, k, v, gates = _inproj(x, mod3, g_mix[l], w_in[l], b_in[l], q_gain[l], k_gain[l])
        hT = _filters(seq, hy_w1[l], hy_b1[l], hy_w2[l], hy_b2[l], hy_w3[l], hy_b3[l], hy_w4[l], hy_freq[l])
        yhyT = _hyena(zhyT, hT, short_w[l], short_b[l], hy_bias[l])
        yat = _attention(q, k, v, q_gain[l], k_gain[l])
        x1, h2, aff = _merge(x, yhyT, yat, gates, mod3, g_ffn[l], w_hy_out[l], w_att_out[l], w_out[l],
                             w_router[l])
        pos = _topk(aff.transpose(0, 2, 1).reshape(bsz * ne, seq), cap).reshape(bsz, ne, seq)
        st16, npass, npass_half = _route_tables(pos)
        xe = _gather(st16, npass_half, pos, h2, cap)
        ye = _moe(xe, w_gate[l].astype(BF16), w_up[l].astype(BF16), w_down[l].astype(BF16))
        x = _combine(st16, npass, pos.transpose(0, 2, 1), aff, ye, x1, mod3)
    return x
```

```python
import functools
import math

import numpy as np
import jax
import jax.numpy as jnp
from jax import lax
from jax.experimental import pallas as pl
from jax.experimental.pallas import tpu as pltpu

F32 = jnp.float32
BF16 = jnp.bfloat16
I32 = jnp.int32
HIGHEST = lax.Precision.HIGHEST

EPS = 1e-6
GRID_W = 64
HY_WIDTH = 512
HY_ORDER = 2
SHORT_K = 3
FILTER_EMB = 33
FILTER_HIDDEN = 64
HY_QUICK_DECAY_PCT = 0.3
HY_GRADUAL_DECAY_PCT = 1.5
HY_DECAY_TARGET = 1e-2
N_HEADS = 8
N_KV_HEADS = 2
HEAD_DIM = 64
ROPE_THETA = 10000.0
N_EXPERTS = 16
EC_FACTOR = 2
HY_IN = 3 * HY_WIDTH
ATT_WIDTH = N_HEADS * HEAD_DIM
KV_WIDTH = N_KV_HEADS * HEAD_DIM

LANES = 128
VMEM_LIMIT = 56 * 1024 * 1024
MXU_TILE = 256
TOKEN_TILE = 256
Q_TILE = 256
KV_CHUNK = 1024
HY_CH_BLOCK = 8
HY_CH_BATCH = 4
FILTER_ROWS = 256
WINDOW = 64
BF16_SUBLANES = 16
GATHER_SPLIT = 2
SAFE_SCORE_BOUND = 48.0
BF16_NORM_SLACK = 1.01
F32_MIN_EXP = -150.0
TOPK_EXP_STEPS = 10
TOPK_VALUE_STEPS = 28
ONES_ROWS = 16
Q_SCALE = HEAD_DIM ** -0.5 * math.log2(math.e)

NT_DIMS = (((1,), (1,)), ((), ()))


def _params(*sem):
    return pltpu.CompilerParams(dimension_semantics=sem, vmem_limit_bytes=VMEM_LIMIT)


def _split2(a):
    hi = a.astype(BF16)
    return hi, (a - hi.astype(F32)).astype(BF16)


def _dot_split_lhs(a, b_bf16):
    hi, lo = _split2(a)
    return (jnp.dot(hi, b_bf16, preferred_element_type=F32)
            + jnp.dot(lo, b_bf16, preferred_element_type=F32))


def _adaln_kernel(c_ref, w_ref, b_ref, o_ref):
    c = c_ref[...]
    s = c * jax.nn.sigmoid(c)
    o_ref[...] = jnp.dot(s, w_ref[...], precision=HIGHEST, preferred_element_type=F32) + b_ref[...]


def _adaln(c, w, b):
    bsz, d = c.shape
    n = w.shape[1]
    rows = 8
    cp = jnp.zeros((rows, d), F32).at[:bsz].set(c)
    tn = 1536
    out = pl.pallas_call(
        _adaln_kernel,
        grid=(n // tn,),
        in_specs=[pl.BlockSpec((rows, d), lambda j: (0, 0)),
                  pl.BlockSpec((d, tn), lambda j: (0, j)),
                  pl.BlockSpec((1, tn), lambda j: (0, j))],
        out_specs=pl.BlockSpec((rows, tn), lambda j: (0, j)),
        out_shape=jax.ShapeDtypeStruct((rows, n), F32),
        compiler_params=_params("parallel"),
        name="adaln",
    )(cp, w, b.reshape(1, n))
    return out[:bsz]


def _inproj_kernel(x_ref, mod_ref, g_ref, whyT_ref, bhy_ref, wr_ref, br_ref, cos_ref, sin_ref,
                   qg_ref, kg_ref, bd_ref, zhyT_ref, q_ref, k_ref, vt_ref, gate_ref):
    x = x_ref[0]
    ms = jnp.mean(x * x, axis=-1, keepdims=True)
    sh1 = mod_ref[0, 0:1, :]
    sc1 = mod_ref[0, 1:2, :]
    h = (x * lax.rsqrt(ms + EPS)) * g_ref[...] * (1.0 + sc1) + sh1
    hb = h.astype(BF16)
    zr = jnp.dot(hb, wr_ref[...], preferred_element_type=F32) + br_ref[...]

    cos = cos_ref[...]
    sin = sin_ref[...]
    bd = bd_ref[...]
    lane = lax.broadcasted_iota(I32, cos.shape, 1)
    first = (lane & (HEAD_DIM // 2 - 1)) < HEAD_DIM // 4

    def norm_rope(u, gain, scale):
        ss = _dot_split_lhs(u * u, bd)
        un = (u * lax.rsqrt(ss * (1.0 / HEAD_DIM) + EPS)) * gain
        rot = jnp.where(first, pltpu.roll(un, LANES - HEAD_DIM // 4, 1), pltpu.roll(un, HEAD_DIM // 4, 1))
        return (un * cos + rot * sin) * scale

    per_group = LANES // HEAD_DIM
    for j in range(ATT_WIDTH // LANES):
        qn = norm_rope(zr[:, j * LANES:(j + 1) * LANES], qg_ref[...], Q_SCALE).astype(BF16)
        for hl in range(per_group):
            q_ref[0, j * per_group + hl] = qn[:, hl * HEAD_DIM:(hl + 1) * HEAD_DIM]
    kn = norm_rope(zr[:, ATT_WIDTH:ATT_WIDTH + KV_WIDTH], kg_ref[...], 1.0).astype(BF16)
    for hl in range(N_KV_HEADS):
        k_ref[0, hl] = kn[:, hl * HEAD_DIM:(hl + 1) * HEAD_DIM]
    vt = zr[:, ATT_WIDTH + KV_WIDTH:ATT_WIDTH + 2 * KV_WIDTH].T.astype(BF16)
    for hl in range(N_KV_HEADS):
        vt_ref[0, hl, :HEAD_DIM, :] = vt[hl * HEAD_DIM:(hl + 1) * HEAD_DIM]
        vt_ref[0, hl, HEAD_DIM:, :] = jnp.ones((ONES_ROWS, vt.shape[1]), BF16)
    gate_ref[0] = jax.nn.sigmoid(zr[:, ATT_WIDTH + 2 * KV_WIDTH:])
    zhyT_ref[0] = lax.dot_general(whyT_ref[...], hb, NT_DIMS, preferred_element_type=F32) + bhy_ref[...]


def _rope_tables(seq):
    rows = seq // GRID_W
    t = np.arange(seq)
    row = (t // GRID_W).astype(np.float32)
    col = (t % GRID_W).astype(np.float32)
    half = HEAD_DIM // 2
    quarter = half // 2
    inv = (ROPE_THETA ** (-np.arange(0, half, 2, dtype=np.float32) / half)).astype(np.float32)
    ang_r = row[:, None] * inv[None, :]
    ang_c = col[:, None] * inv[None, :]
    ang = np.concatenate([ang_r, ang_r, ang_c, ang_c], axis=1)
    sign = np.concatenate([-np.ones(quarter), np.ones(quarter)] * 2).astype(np.float32)
    cos = np.cos(ang).astype(np.float32)
    sin = (np.sin(ang) * sign[None, :]).astype(np.float32)
    reps = LANES // HEAD_DIM
    del rows
    return jnp.asarray(np.tile(cos, (1, reps))), jnp.asarray(np.tile(sin, (1, reps)))


def _inproj(x, mod3, g_mix, w_in, b_in, q_gain, k_gain):
    bsz, seq, d = x.shape
    tm = TOKEN_TILE
    n_rest = w_in.shape[1] - HY_IN
    n_gate = n_rest - ATT_WIDTH - 2 * KV_WIDTH
    whyT = w_in[:, :HY_IN].T.astype(BF16)
    wr = w_in[:, HY_IN:].astype(BF16)
    bhy = b_in[:HY_IN].reshape(HY_IN, 1)
    br = b_in[HY_IN:].reshape(1, n_rest)
    cos, sin = _rope_tables(seq)
    reps = LANES // HEAD_DIM
    qg = jnp.tile(q_gain, reps).reshape(1, LANES)
    kg = jnp.tile(k_gain, reps).reshape(1, LANES)
    bd = jnp.asarray(np.kron(np.eye(reps, dtype=np.float32), np.ones((HEAD_DIM, HEAD_DIM), np.float32))).astype(BF16)
    full = lambda shape: pl.BlockSpec(shape, lambda b, i: (0,) * len(shape))
    return pl.pallas_call(
        _inproj_kernel,
        grid=(bsz, seq // tm),
        in_specs=[pl.BlockSpec((1, tm, d), lambda b, i: (b, i, 0)),
                  pl.BlockSpec((1, 6, d), lambda b, i: (b, 0, 0)),
                  full((1, d)), full((HY_IN, d)), full((HY_IN, 1)), full((d, n_rest)), full((1, n_rest)),
                  pl.BlockSpec((tm, LANES), lambda b, i: (i, 0)),
                  pl.BlockSpec((tm, LANES), lambda b, i: (i, 0)),
                  full((1, LANES)), full((1, LANES)), full((LANES, LANES))],
        out_specs=[pl.BlockSpec((1, HY_IN, tm), lambda b, i: (b, 0, i)),
                   pl.BlockSpec((1, N_HEADS, tm, HEAD_DIM), lambda b, i: (b, 0, i, 0)),
                   pl.BlockSpec((1, N_KV_HEADS, tm, HEAD_DIM), lambda b, i: (b, 0, i, 0)),
                   pl.BlockSpec((1, N_KV_HEADS, HEAD_DIM + ONES_ROWS, tm), lambda b, i: (b, 0, 0, i)),
                   pl.BlockSpec((1, tm, n_gate), lambda b, i: (b, i, 0))],
        out_shape=[jax.ShapeDtypeStruct((bsz, HY_IN, seq), F32),
                   jax.ShapeDtypeStruct((bsz, N_HEADS, seq, HEAD_DIM), BF16),
                   jax.ShapeDtypeStruct((bsz, N_KV_HEADS, seq, HEAD_DIM), BF16),
                   jax.ShapeDtypeStruct((bsz, N_KV_HEADS, HEAD_DIM + ONES_ROWS, seq), BF16),
                   jax.ShapeDtypeStruct((bsz, seq, n_gate), F32)],
        compiler_params=_params("parallel", "parallel"),
        name="inproj",
    )(x, mod3, g_mix.reshape(1, d), whyT, bhy, wr, br, cos, sin, qg, kg, bd)


def _filter_kernel(zT_ref, w1T_ref, b1_ref, w2T_ref, b2_ref, w3T_ref, b3_ref, fr_ref, w4T_ref,
                   t01_ref, absd_ref, o_ref, hdn_ref):
    @pl.when(pl.program_id(0) == 0)
    def _():
        fr = fr_ref[...]
        h = jnp.sin(fr[:, 0:1] * (jnp.dot(w1T_ref[...], zT_ref[...], precision=HIGHEST,
                                          preferred_element_type=F32) + b1_ref[...]))
        h = jnp.sin(fr[:, 1:2] * (jnp.dot(w2T_ref[...], h, precision=HIGHEST,
                                          preferred_element_type=F32) + b2_ref[...]))
        h = jnp.sin(fr[:, 2:3] * (jnp.dot(w3T_ref[...], h, precision=HIGHEST,
                                          preferred_element_type=F32) + b3_ref[...]))
        hdn_ref[...] = h

    w_hi, w_lo = _split2(w4T_ref[...])
    h_hi, h_lo = _split2(hdn_ref[...])
    hT = (jnp.dot(w_hi, h_hi, preferred_element_type=F32) + jnp.dot(w_lo, h_hi, preferred_element_type=F32)
          + jnp.dot(w_hi, h_lo, preferred_element_type=F32))
    hT = hT * jnp.exp(-t01_ref[...] * absd_ref[...])
    nrm = jnp.sum(jnp.abs(hT), axis=1, keepdims=True) + EPS
    o_ref[...] = hT * (1.0 / nrm)


def _filters(seq, w1, b1, w2, b2, w3, b3, w4, freq):
    emb_pad = 64
    bands = (FILTER_EMB - 1) // 2
    t01 = np.linspace(0.0, 1.0, seq, dtype=np.float32)[None, :]
    f = np.linspace(1e-4, bands - 1, bands, dtype=np.float32)[:, None]
    w = ((2.0 * math.pi) * np.arange(seq, dtype=np.float32) / seq).astype(np.float32)[None, :]
    zT = np.zeros((emb_pad, seq), np.float32)
    zT[0:1] = t01
    zT[1:1 + bands] = np.cos(f * w)
    zT[1 + bands:1 + 2 * bands] = -np.sin(f * w)
    w1T = jnp.zeros((FILTER_HIDDEN, emb_pad), F32).at[:, :FILTER_EMB].set(w1.T)
    max_decay = math.log(HY_DECAY_TARGET) / HY_QUICK_DECAY_PCT
    min_decay = math.log(HY_DECAY_TARGET) / HY_GRADUAL_DECAY_PCT
    deltas = np.abs(np.linspace(min_decay, max_decay, HY_WIDTH, dtype=np.float32))
    n_rows = HY_ORDER * 2 * HY_WIDTH
    absd = np.tile(deltas, HY_ORDER * 2).reshape(n_rows, 1)
    rb = FILTER_ROWS
    full = lambda shape: pl.BlockSpec(shape, lambda j: (0,) * len(shape))
    col = lambda v: v.reshape(FILTER_HIDDEN, 1)
    return pl.pallas_call(
        _filter_kernel,
        grid=(n_rows // rb,),
        in_specs=[full((emb_pad, seq)),
                  full((FILTER_HIDDEN, emb_pad)), full((FILTER_HIDDEN, 1)),
                  full((FILTER_HIDDEN, FILTER_HIDDEN)), full((FILTER_HIDDEN, 1)),
                  full((FILTER_HIDDEN, FILTER_HIDDEN)), full((FILTER_HIDDEN, 1)),
                  full((FILTER_HIDDEN, 3)),
                  pl.BlockSpec((rb, FILTER_HIDDEN), lambda j: (j, 0)),
                  full((1, seq)),
                  pl.BlockSpec((rb, 1), lambda j: (j, 0))],
        out_specs=pl.BlockSpec((rb, seq), lambda j: (j, 0)),
        out_shape=jax.ShapeDtypeStruct((n_rows, seq), F32),
        scratch_shapes=[pltpu.VMEM((FILTER_HIDDEN, seq), F32)],
        compiler_params=_params("arbitrary"),
        name="filters",
    )(jnp.asarray(zT), w1T, col(b1), w2.T, col(b2), w3.T, col(b3), freq.T, w4.T,
      jnp.asarray(t01), jnp.asarray(absd))


def _dft_tables(seq):
    n = 2 * seq
    r = int(round(math.sqrt(n)))
    assert r * r == n, "2*seq must be a perfect square"
    hh = r // 2
    rh = -(-(hh + 1) // BF16_SUBLANES) * BF16_SUBLANES
    k = np.arange(r, dtype=np.float64)
    ang = 2.0 * np.pi * np.outer(k, k) / r
    fr, fi = np.cos(ang), -np.sin(ang)
    angt = 2.0 * np.pi * np.outer(k, k) / n
    keep = np.zeros((rh, 1))
    keep[:hh + 1] = 1.0
    pad = lambda a: np.concatenate([a[:hh + 1], np.zeros((rh - hh - 1,) + a.shape[1:])], axis=0)
    twr, twi = pad(np.cos(angt)), pad(-np.sin(angt))
    f1 = np.concatenate([pad(fr[:, :hh]), pad(fi[:, :hh])], axis=0)
    f3 = np.block([[fr, fi], [-fi, fr]])
    f3i = np.block([[fr, -fi], [fi, fr]])
    weight = np.full((rh, 1), 2.0) * keep
    weight[0] = weight[hh] = 1.0
    f1i = np.concatenate([(pad(fr[:, :hh]) * weight).T, (pad(fi[:, :hh]) * weight).T], axis=1) / n
    as32 = lambda a: jnp.asarray(a.astype(np.float32))
    return as32(f1), as32(twr), as32(twi), as32(f3), as32(f3i), as32(f1i)


def _hyena_kernel(sw_ref, sb_ref, fb_ref, v_ref, x1_ref, x2_ref, h_ref, f1_f32, twr_ref, twi_ref,
                  f3_f32, f3i_f32, f1i_f32, o_ref, f1_ref, f3_ref, f3i_ref, f1i_ref):
    bsz, cb, hh, r = v_ref.shape
    rh = twr_ref.shape[0]
    j = pl.program_id(0)
    f1_ref[...] = f1_f32[...].astype(BF16)
    f3_ref[...] = f3_f32[...].astype(BF16)
    f3i_ref[...] = f3i_f32[...].astype(BF16)
    f1i_ref[...] = f1i_f32[...].astype(BF16)
    twr = twr_ref[...]
    twi = twi_ref[...]
    lane = lax.broadcasted_iota(I32, (hh, r), 1)
    row = lax.broadcasted_iota(I32, (hh, r), 0)

    def shift_prev(z):
        a = pltpu.roll(z, 1, 1)
        a = jnp.where(lane == 0, pltpu.roll(a, 1, 0), a)
        return jnp.where((lane == 0) & (row == 0), 0.0, a)

    def shift_next(z):
        a = pltpu.roll(z, r - 1, 1)
        a = jnp.where(lane == r - 1, pltpu.roll(a, hh - 1, 0), a)
        return jnp.where((lane == r - 1) & (row == hh - 1), 0.0, a)

    def fwd(slabs):
        m = len(slabs)
        acat = jnp.concatenate([a.astype(BF16) for a in slabs], axis=1)
        s = jnp.dot(f1_ref[...], acat, preferred_element_type=F32)
        lhs = []
        for i in range(m):
            br = s[:rh, i * r:(i + 1) * r]
            bi = s[rh:, i * r:(i + 1) * r]
            cr = br * twr - bi * twi
            ci = br * twi + bi * twr
            lhs.append(jnp.concatenate([cr, ci], axis=1).astype(BF16))
        d = jnp.dot(jnp.concatenate(lhs, axis=0), f3_ref[...], preferred_element_type=F32)
        return [(d[i * rh:(i + 1) * rh, :r], d[i * rh:(i + 1) * rh, r:]) for i in range(m)]

    def inv(specs):
        m = len(specs)
        lhs = jnp.concatenate([jnp.concatenate([pr, pi], axis=1).astype(BF16) for pr, pi in specs], axis=0)
        e = jnp.dot(lhs, f3i_ref[...], preferred_element_type=F32)
        rhs = []
        for i in range(m):
            er = e[i * rh:(i + 1) * rh, :r]
            ei = e[i * rh:(i + 1) * rh, r:]
            tr = er * twr + ei * twi
            ti = ei * twr - er * twi
            rhs.append(jnp.concatenate([tr, ti], axis=0).astype(BF16))
        y = jnp.dot(f1i_ref[...], jnp.concatenate(rhs, axis=1), preferred_element_type=F32)
        return [y[:, i * r:(i + 1) * r] for i in range(m)]

    def cmul(a, g):
        return a[0] * g[0] - a[1] * g[1], a[0] * g[1] + a[1] * g[0]

    def sconv(ref, b, c, gi):
        z = ref[b, c]
        return (sb_ref[gi] + sw_ref[gi] * shift_prev(z) + sw_ref[HY_IN + gi] * z
                + sw_ref[2 * HY_IN + gi] * shift_next(z))

    nb = HY_CH_BATCH

    def body(t, carry):
        cs = [t * nb + i for i in range(nb)]
        chs = [j * cb + c for c in cs]
        pairs = [(c, ch, b) for c, ch in zip(cs, chs) for b in range(bsz)]
        v = [sconv(v_ref, b, c, ch) for c, ch, b in pairs]
        filt = [h_ref[o, d, c] for c in cs for o in range(HY_ORDER) for d in range(2)]
        spec = fwd(v + filt)
        nv = len(v)
        g = []
        for i in range(nb):
            gi = []
            for o in range(HY_ORDER):
                hf = spec[nv + (i * HY_ORDER + o) * 2]
                hb = spec[nv + (i * HY_ORDER + o) * 2 + 1]
                gi.append((hf[0] + hb[0], hf[1] - hb[1]))
            g.append(gi)
        conv = inv([cmul(spec[n], g[n // bsz][0]) for n in range(nv)])
        y = [sconv(x1_ref, b, c, HY_WIDTH + ch) * (conv[n] + fb_ref[ch] * v[n])
             for n, (c, ch, b) in enumerate(pairs)]
        spec = fwd(y)
        conv = inv([cmul(spec[n], g[n // bsz][1]) for n in range(nv)])
        for n, (c, ch, b) in enumerate(pairs):
            o_ref[b, c] = (sconv(x2_ref, b, c, 2 * HY_WIDTH + ch)
                           * (conv[n] + fb_ref[HY_WIDTH + ch] * y[n]))
        return carry

    lax.fori_loop(0, cb // nb, body, 0)


def _hyena(zhyT, hT, short_w, short_b, filt_bias):
    bsz, _, seq = zhyT.shape
    tabs = _dft_tables(seq)
    r = tabs[1].shape[1]
    hh = r // 2
    z4 = zhyT.reshape(bsz, HY_IN, hh, r)
    h5 = hT.reshape(HY_ORDER, 2, HY_WIDTH, hh, r)
    cb = HY_CH_BLOCK
    nblk = HY_WIDTH // cb
    smem = pl.BlockSpec(memory_space=pltpu.SMEM)
    full = lambda a: pl.BlockSpec(a.shape, lambda j: (0,) * a.ndim)
    zspec = lambda off: pl.BlockSpec((bsz, cb, hh, r), lambda j: (0, off * nblk + j, 0, 0))
    out = pl.pallas_call(
        _hyena_kernel,
        grid=(nblk,),
        in_specs=[smem, smem, smem, zspec(0), zspec(1), zspec(2),
                  pl.BlockSpec((HY_ORDER, 2, cb, hh, r), lambda j: (0, 0, j, 0, 0))]
                 + [full(t) for t in tabs],
        out_specs=pl.BlockSpec((bsz, cb, hh, r), lambda j: (0, j, 0, 0)),
        out_shape=jax.ShapeDtypeStruct((bsz, HY_WIDTH, hh, r), F32),
        scratch_shapes=[pltpu.VMEM(tabs[i].shape, BF16) for i in (0, 3, 4, 5)],
        compiler_params=_params("parallel"),
        name="hyena",
    )(short_w.reshape(-1), short_b, filt_bias.reshape(-1), z4, z4, z4, h5, *tabs)
    return out.reshape(bsz, HY_WIDTH, seq)


def _attn_kernel(bounded_ref, q_ref, k_ref, vt_ref, o_ref):
    g, tq, hd = q_ref.shape[1:]
    seq = k_ref.shape[2]
    rows = vt_ref.shape[2]
    n_chunks = seq // KV_CHUNK
    q = q_ref[0].reshape(g * tq, hd)

    def scores(kc):
        kb = k_ref[0, 0, kc * KV_CHUNK:(kc + 1) * KV_CHUNK, :]
        return lax.dot_general(kb, q, NT_DIMS, preferred_element_type=F32)

    def values(kc):
        return vt_ref[0, 0, :, kc * KV_CHUNK:(kc + 1) * KV_CHUNK]

    def finish(acc):
        out = acc[:hd] / acc[hd:hd + 1]
        o_ref[0] = out.T.reshape(g, tq, hd).astype(o_ref.dtype)

    @pl.when(bounded_ref[0] == 1)
    def _():
        acc = jnp.zeros((rows, g * tq), F32)
        for kc in range(n_chunks):
            p = jnp.exp2(scores(kc)).astype(BF16)
            acc = acc + jnp.dot(values(kc), p, preferred_element_type=F32)
        finish(acc)

    @pl.when(bounded_ref[0] != 1)
    def _():
        m = jnp.full((1, g * tq), -jnp.inf, F32)
        acc = jnp.zeros((rows, g * tq), F32)
        s_next = scores(0)
        for kc in range(n_chunks):
            s = s_next
            if kc + 1 < n_chunks:
                s_next = scores(kc + 1)
            m_new = jnp.maximum(m, jnp.max(s, axis=0, keepdims=True))
            p = jnp.exp2(s - m_new).astype(BF16)
            acc = jnp.exp2(m - m_new) * acc + jnp.dot(values(kc), p, preferred_element_type=F32)
            m = m_new
        finish(acc)


def _attention(qh, kh, vt, q_gain, k_gain):
    bsz, _, seq, _ = qh.shape
    g = N_HEADS // N_KV_HEADS
    rows = vt.shape[2]
    bound = HEAD_DIM * Q_SCALE * BF16_NORM_SLACK * jnp.max(jnp.abs(q_gain)) * jnp.max(jnp.abs(k_gain))
    bounded = (bound <= SAFE_SCORE_BOUND).astype(I32).reshape(1)
    tq = Q_TILE
    grid_spec = pltpu.PrefetchScalarGridSpec(
        num_scalar_prefetch=1,
        grid=(bsz, N_KV_HEADS, seq // tq),
        in_specs=[pl.BlockSpec((1, g, tq, HEAD_DIM), lambda b, h, i, f: (b, h, i, 0)),
                  pl.BlockSpec((1, 1, seq, HEAD_DIM), lambda b, h, i, f: (b, h, 0, 0)),
                  pl.BlockSpec((1, 1, rows, seq), lambda b, h, i, f: (b, h, 0, 0))],
        out_specs=pl.BlockSpec((1, g, tq, HEAD_DIM), lambda b, h, i, f: (b, h, i, 0)))
    return pl.pallas_call(
        _attn_kernel,
        grid_spec=grid_spec,
        out_shape=jax.ShapeDtypeStruct((bsz, N_HEADS, seq, HEAD_DIM), BF16),
        compiler_params=_params("parallel", "parallel", "parallel"),
        name="attention",
    )(bounded, qh, kh, vt)


def _merge_kernel(x_ref, yhyT_ref, yat_ref, gate_ref, mod_ref, g_ref, why_ref, wat_ref, wout_ref, wr_ref,
                  x1_ref, h2_ref, aff_ref):
    d = x_ref.shape[2]
    yhy = yhyT_ref[0].T.astype(BF16)
    a = jnp.dot(yhy, why_ref[...], preferred_element_type=F32)
    yat = jnp.concatenate([yat_ref[0, h] for h in range(yat_ref.shape[1])], axis=1)
    b = jnp.dot(yat, wat_ref[...], preferred_element_type=F32)
    gate = gate_ref[0]
    mrg = gate[:, :d] * a + gate[:, d:] * b
    o = jnp.dot(mrg.astype(BF16), wout_ref[...], preferred_element_type=F32)
    gt1 = mod_ref[0, 2:3, :]
    sh2 = mod_ref[0, 3:4, :]
    sc2 = mod_ref[0, 4:5, :]
    x1 = x_ref[0] + gt1 * o
    x1_ref[0] = x1
    ms = jnp.mean(x1 * x1, axis=-1, keepdims=True)
    h2 = (x1 * lax.rsqrt(ms + EPS)) * g_ref[...] * (1.0 + sc2) + sh2
    hi, lo = _split2(h2)
    h2_ref[0] = hi
    w_hi, w_lo = _split2(wr_ref[...])
    logits = (jnp.dot(hi, w_hi, preferred_element_type=F32)
              + jnp.dot(lo, w_hi, preferred_element_type=F32)
              + jnp.dot(hi, w_lo, preferred_element_type=F32))
    ex = jnp.exp(logits - jnp.max(logits, axis=-1, keepdims=True))
    aff_ref[0] = ex / jnp.sum(ex, axis=-1, keepdims=True)


def _merge(x, yhyT, yat, gates, mod3, g_ffn, w_hy_out, w_att_out, w_out, w_router):
    bsz, seq, d = x.shape
    tm = TOKEN_TILE
    ne = w_router.shape[1]
    full = lambda shape: pl.BlockSpec(shape, lambda b, i: (0,) * len(shape))
    return pl.pallas_call(
        _merge_kernel,
        grid=(bsz, seq // tm),
        in_specs=[pl.BlockSpec((1, tm, d), lambda b, i: (b, i, 0)),
                  pl.BlockSpec((1, HY_WIDTH, tm), lambda b, i: (b, 0, i)),
                  pl.BlockSpec((1, N_HEADS, tm, HEAD_DIM), lambda b, i: (b, 0, i, 0)),
                  pl.BlockSpec((1, tm, 2 * d), lambda b, i: (b, i, 0)),
                  pl.BlockSpec((1, 6, d), lambda b, i: (b, 0, 0)),
                  full((1, d)), full((HY_WIDTH, d)), full((ATT_WIDTH, d)), full((d, d)), full((d, ne))],
        out_specs=[pl.BlockSpec((1, tm, d), lambda b, i: (b, i, 0)),
                   pl.BlockSpec((1, tm, d), lambda b, i: (b, i, 0)),
                   pl.BlockSpec((1, tm, ne), lambda b, i: (b, i, 0))],
        out_shape=[jax.ShapeDtypeStruct((bsz, seq, d), F32),
                   jax.ShapeDtypeStruct((bsz, seq, d), BF16),
                   jax.ShapeDtypeStruct((bsz, seq, ne), F32)],
        compiler_params=_params("parallel", "parallel"),
        name="merge",
    )(x, yhyT, yat, gates, mod3, g_ffn.reshape(1, d), w_hy_out.astype(BF16), w_att_out.astype(BF16),
      w_out.astype(BF16), w_router)


def _topk_kernel(a_ref, ut_ref, pos_ref, *, cap):
    nr, seq = a_ref.shape
    a = a_ref[...]

    def count(mask):
        return jnp.sum(mask.astype(F32), axis=1, keepdims=True)

    def probe(t, lo, hi):
        ok = count(a >= t) >= cap
        return ok, jnp.where(ok, t, lo), jnp.where(ok, hi, t)

    def by_exponent(_, c):
        elo, ehi, lo, hi = c
        emid = 0.5 * (elo + ehi)
        ok, lo, hi = probe(jnp.exp2(emid), lo, hi)
        return jnp.where(ok, emid, elo), jnp.where(ok, ehi, emid), lo, hi

    def by_value(_, c):
        lo, hi = c
        _, lo, hi = probe(0.5 * (lo + hi), lo, hi)
        return lo, hi

    col = lambda v: jnp.full((nr, 1), v, F32)
    _, _, lo, hi = lax.fori_loop(0, TOPK_EXP_STEPS, by_exponent, (col(F32_MIN_EXP), col(1.0), col(0.0), col(2.0)))
    lo, hi = lax.fori_loop(0, TOPK_VALUE_STEPS, by_value, (lo, hi))
    gt = a >= hi
    eq = (a >= lo) & (a < hi)
    need = cap - count(gt)
    ut = ut_ref[...]
    run_eq = jnp.zeros((nr, 1), F32)
    run_sel = jnp.zeros((nr, 1), F32)
    for ch in range(seq // LANES):
        sl = slice(ch * LANES, (ch + 1) * LANES)
        eq_rank = jnp.dot(eq[:, sl].astype(BF16), ut, preferred_element_type=F32) + run_eq
        run_eq = eq_rank[:, LANES - 1:LANES]
        sel = gt[:, sl] | (eq[:, sl] & (eq_rank <= need))
        pos = jnp.dot(sel.astype(BF16), ut, preferred_element_type=F32) + run_sel
        run_sel = pos[:, LANES - 1:LANES]
        pos_ref[:, sl] = jnp.where(sel, pos.astype(I32) - 1, -1)


def _topk(aff2, cap):
    nr, seq = aff2.shape
    ut = jnp.asarray(np.triu(np.ones((LANES, LANES), np.float32))).astype(BF16)
    return pl.pallas_call(
        functools.partial(_topk_kernel, cap=cap),
        grid=(1,),
        in_specs=[pl.BlockSpec((nr, seq), lambda i: (0, 0)), pl.BlockSpec((LANES, LANES), lambda i: (0, 0))],
        out_specs=pl.BlockSpec((nr, seq), lambda i: (0, 0)),
        out_shape=jax.ShapeDtypeStruct((nr, seq), I32),
        compiler_params=_params("arbitrary"),
        name="topk",
    )(aff2, ut)


def _window_start(st_ref, idx, w, cap):
    base = st_ref[idx] + w * WINDOW
    return base, pl.multiple_of(jnp.minimum(base, cap - WINDOW), BF16_SUBLANES)


def _gather_kernel(st_ref, np_ref, pos_ref, h2_ref, xe_ref):
    eh, tile = pos_ref.shape[1:]
    cap = xe_ref.shape[2]
    nt = pl.num_programs(2)
    b, half, i = pl.program_id(0), pl.program_id(1), pl.program_id(2)
    ne = pl.num_programs(1) * eh

    @pl.when(i == 0)
    def _():
        xe_ref[...] = jnp.zeros_like(xe_ref)

    rho = lax.broadcasted_iota(I32, (WINDOW, tile), 0)

    def one_pass(w, carry):
        starts, rows = [], []
        for el in range(eh):
            base, st = _window_start(st_ref, (b * ne + half * eh + el) * nt + i, w, cap)
            pos = pos_ref[0, el:el + 1, :]
            rows.append(((pos - st == rho) & (pos >= base)).astype(BF16))
            starts.append(st)
        got = jnp.dot(jnp.concatenate(rows, axis=0), h2_ref[0], preferred_element_type=F32).astype(BF16)
        for el in range(eh):
            xe_ref[0, el, pl.ds(starts[el], WINDOW), :] += got[el * WINDOW:(el + 1) * WINDOW]
        return carry

    lax.fori_loop(0, np_ref[(b * pl.num_programs(1) + half) * nt + i], one_pass, 0)


def _gather(st16, npass_half, pos, h2, cap):
    bsz, ne, seq = pos.shape
    d = h2.shape[2]
    tile = TOKEN_TILE
    eh = ne // GATHER_SPLIT
    grid_spec = pltpu.PrefetchScalarGridSpec(
        num_scalar_prefetch=2,
        grid=(bsz, GATHER_SPLIT, seq // tile),
        in_specs=[pl.BlockSpec((1, eh, tile), lambda b, h, i, st, n: (b, h, i)),
                  pl.BlockSpec((1, tile, d), lambda b, h, i, st, n: (b, i, 0))],
        out_specs=pl.BlockSpec((1, eh, cap, d), lambda b, h, i, st, n: (b, h, 0, 0)))
    return pl.pallas_call(
        _gather_kernel,
        grid_spec=grid_spec,
        out_shape=jax.ShapeDtypeStruct((bsz, ne, cap, d), BF16),
        compiler_params=_params("parallel", "parallel", "arbitrary"),
        name="gather",
    )(st16, npass_half, pos, h2)


def _moe_kernel(xe_ref, wg_ref, wu_ref, wd_ref, ye_ref, wg_s, wu_s, wd_s):
    tile = TOKEN_TILE
    k = pl.program_id(0)
    b = pl.program_id(1)
    ne = pl.num_programs(0) - 1
    rows_in = wg_ref.shape[1]
    rows_ff = wd_ref.shape[1]

    @pl.when(k < ne)
    def _():
        nxt = k % 2
        wg_s[nxt, pl.ds(pl.multiple_of(b * rows_in, rows_in), rows_in), :] = wg_ref[0].astype(BF16)
        wu_s[nxt, pl.ds(pl.multiple_of(b * rows_in, rows_in), rows_in), :] = wu_ref[0].astype(BF16)
        wd_s[nxt, pl.ds(pl.multiple_of(b * rows_ff, rows_ff), rows_ff), :] = wd_ref[0].astype(BF16)

    @pl.when(k == 0)
    def _():
        ye_ref[...] = jnp.zeros_like(ye_ref)

    @pl.when(k > 0)
    def _():
        cur = (k + 1) % 2
        for r in range(xe_ref.shape[2] // tile):
            xe = xe_ref[0, 0, r * tile:(r + 1) * tile, :]
            a = jnp.dot(xe, wg_s[cur], preferred_element_type=F32)
            u = jnp.dot(xe, wu_s[cur], preferred_element_type=F32)
            hmid = (a * jax.nn.sigmoid(a) * u).astype(BF16)
            ye_ref[0, 0, r * tile:(r + 1) * tile, :] = jnp.dot(hmid, wd_s[cur],
                                                              preferred_element_type=F32).astype(BF16)


def _moe(xe, wg, wu, wd):
    bsz, ne, cap, d = xe.shape
    dff = wg.shape[2]
    assert d % bsz == 0 and dff % bsz == 0
    prev = lambda k: jnp.maximum(k - 1, 0)
    this = lambda k: jnp.minimum(k, ne - 1)
    return pl.pallas_call(
        _moe_kernel,
        grid=(ne + 1, bsz),
        in_specs=[pl.BlockSpec((1, 1, cap, d), lambda k, b: (b, prev(k), 0, 0)),
                  pl.BlockSpec((1, d // bsz, dff), lambda k, b: (this(k), b, 0)),
                  pl.BlockSpec((1, d // bsz, dff), lambda k, b: (this(k), b, 0)),
                  pl.BlockSpec((1, dff // bsz, d), lambda k, b: (this(k), b, 0))],
        out_specs=pl.BlockSpec((1, 1, cap, d), lambda k, b: (b, jnp.where(k == 0, ne, k - 1), 0, 0)),
        out_shape=jax.ShapeDtypeStruct((bsz, ne + 1, cap, d), BF16),
        scratch_shapes=[pltpu.VMEM((2, d, dff), BF16), pltpu.VMEM((2, d, dff), BF16),
                        pltpu.VMEM((2, dff, d), BF16)],
        compiler_params=_params("arbitrary", "arbitrary"),
        name="moe",
    )(xe, wg, wu, wd)


def _combine_kernel(st_ref, np_ref, pos_ref, aff_ref, ye_ref, x1_ref, mod_ref, ex_ref, rho_ref, o_ref,
                    rhs_ref, acc_ref):
    tile, ne = pos_ref.shape[1:]
    cap = ye_ref.shape[2]
    nt = pl.num_programs(1)
    b, i = pl.program_id(0), pl.program_id(1)
    acc_ref[...] = jnp.zeros_like(acc_ref)
    lane = lax.broadcasted_iota(I32, (1, ne), 1)
    pos = pos_ref[0]
    aexp = jnp.dot(aff_ref[0].astype(BF16), ex_ref[...], preferred_element_type=F32)

    def one_pass(w, carry):
        base_v = jnp.zeros((1, ne), I32)
        st_v = jnp.zeros((1, ne), I32)
        for e in range(ne):
            base, st = _window_start(st_ref, (b * ne + e) * nt + i, w, cap)
            rhs_ref[e * WINDOW:(e + 1) * WINDOW, :] = ye_ref[0, e, pl.ds(st, WINDOW), :]
            base_v = jnp.where(lane == e, base, base_v)
            st_v = jnp.where(lane == e, st, st_v)
        off = pos - st_v
        off = jnp.where((pos >= base_v) & (off >= 0) & (off < WINDOW), off, -1)
        oexp = jnp.dot(off.astype(F32).astype(BF16), ex_ref[...], preferred_element_type=F32)
        pt = jnp.where(oexp == rho_ref[...], aexp, 0.0).astype(BF16)
        acc_ref[...] += jnp.dot(pt, rhs_ref[...], preferred_element_type=F32)
        return carry

    lax.fori_loop(0, np_ref[b * nt + i], one_pass, 0)
    o_ref[0] = x1_ref[0] + mod_ref[0, 5:6, :] * acc_ref[...]


def _combine(st16, npass, pos_tok, aff_tok, ye, x1, mod3):
    bsz, seq, d = x1.shape
    ne = aff_tok.shape[2]
    cap = ye.shape[2]
    tile = TOKEN_TILE
    spread =np.kron(np.eye(ne, dtype=np.float32), np.ones((1, WINDOW), np.float32))
    rho = np.tile(np.arange(WINDOW, dtype=np.float32), ne).reshape(1, ne * WINDOW)
    grid_spec = pltpu.PrefetchScalarGridSpec(
        num_scalar_prefetch=2,
        grid=(bsz, seq // tile),
        in_specs=[pl.BlockSpec((1, tile, ne), lambda b, i, st, n: (b, i, 0)),
                  pl.BlockSpec((1, tile, ne), lambda b, i, st, n: (b, i, 0)),
                  pl.BlockSpec((1, ne, cap, d), lambda b, i, st, n: (b, 0, 0, 0), pipeline_mode=pl.Buffered(1)),
                  pl.BlockSpec((1, tile, d), lambda b, i, st, n: (b, i, 0)),
                  pl.BlockSpec((1, 6, d), lambda b, i, st, n: (b, 0, 0)),
                  pl.BlockSpec((ne, ne * WINDOW), lambda b, i, st, n: (0, 0)),
                  pl.BlockSpec((1, ne * WINDOW), lambda b, i, st, n: (0, 0))],
        out_specs=pl.BlockSpec((1, tile, d), lambda b, i, st, n: (b, i, 0)),
        scratch_shapes=[pltpu.VMEM((ne * WINDOW, d), BF16), pltpu.VMEM((tile, d), F32)])
    return pl.pallas_call(
        _combine_kernel,
        grid_spec=grid_spec,
        out_shape=jax.ShapeDtypeStruct((bsz, seq, d), F32),
        compiler_params=_params("parallel", "arbitrary"),
        name="combine",
    )(st16, npass, pos_tok, aff_tok, ye, x1, mod3, jnp.asarray(spread).astype(BF16), jnp.asarray(rho))


def _route_tables(pos):
    bsz, ne, seq = pos.shape
    tile = TOKEN_TILE
    counts = jnp.sum((pos >= 0).reshape(bsz, ne, seq // tile, tile), axis=-1, dtype=I32)
    starts = jnp.cumsum(counts, axis=-1) - counts
    st16 = (starts // BF16_SUBLANES) * BF16_SUBLANES
    span = jnp.where(counts > 0, starts - st16 + counts, 0)
    passes = (span + WINDOW - 1) // WINDOW
    npass = jnp.max(passes, axis=1)
    npass_half = jnp.max(passes.reshape(bsz, GATHER_SPLIT, ne // GATHER_SPLIT, -1), axis=2)
    return st16.reshape(-1), npass.reshape(-1), npass_half.reshape(-1)


def kernel(x, c, w_ada, b_ada, g_mix, g_ffn, w_in, b_in, short_w, short_b, hy_w1, hy_b1, hy_w2, hy_b2, hy_w3, hy_b3, hy_w4, hy_freq, hy_bias, q_gain, k_gain, w_hy_out, w_att_out, w_out, w_router, w_gate, w_up, w_down):
    bsz, seq, d = x.shape
    depth = w_ada.shape[0]
    ne = w_router.shape[-1]
    cap = EC_FACTOR * seq // ne
    for l in range(depth):
        mod3 = _adaln(c, w_ada[l], b_ada[l]).reshape(bsz, 6, d)
        zhyT, q, k, v, gates = _inproj(x, mod3, g_mix[l], w_in[l], b_in[l], q_gain[l], k_gain[l])
        hT = _filters(seq, hy_w1[l], hy_b1[l], hy_w2[l], hy_b2[l], hy_w3[l], hy_b3[l], hy_w4[l], hy_freq[l])
        yhyT = _hyena(zhyT, hT, short_w[l], short_b[l], hy_bias[l])
        yat = _attention(q, k, v, q_gain[l], k_gain[l])
        x1, h2, aff = _merge(x, yhyT, yat, gates, mod3, g_ffn[l], w_hy_out[l], w_att_out[l], w_out[l],
                             w_router[l])
        pos = _topk(aff.transpose(0, 2, 1).reshape(bsz * ne, seq), cap).reshape(bsz, ne, seq)
        st16, npass, npass_half = _route_tables(pos)
        xe = _gather(st16, npass_half, pos, h2, cap)
        ye = _moe(xe, w_gate[l], w_up[l], w_down[l])
        x = _combine(st16, npass, pos.transpose(0, 2, 1), aff, ye, x1, mod3)
    return x
```

```python
import functools
import math

import numpy as np
import jax
import jax.numpy as jnp
from jax import lax
from jax.experimental import pallas as pl
from jax.experimental.pallas import tpu as pltpu

F32 = jnp.float32
BF16 = jnp.bfloat16
I32 = jnp.int32
HIGHEST = lax.Precision.HIGHEST

EPS = 1e-6
GRID_W = 64
HY_WIDTH = 512
HY_ORDER = 2
SHORT_K = 3
FILTER_EMB = 33
FILTER_HIDDEN = 64
HY_QUICK_DECAY_PCT = 0.3
HY_GRADUAL_DECAY_PCT = 1.5
HY_DECAY_TARGET = 1e-2
N_HEADS = 8
N_KV_HEADS = 2
HEAD_DIM = 64
ROPE_THETA = 10000.0
N_EXPERTS = 16
EC_FACTOR = 2
HY_IN = 3 * HY_WIDTH
ATT_WIDTH = N_HEADS * HEAD_DIM
KV_WIDTH = N_KV_HEADS * HEAD_DIM

LANES = 128
VMEM_LIMIT = 56 * 1024 * 1024
MXU_TILE = 256
TOKEN_TILE = 256
INPROJ_TILE = 512
Q_TILE = 256
KV_CHUNK = 2048
HY_CH_BLOCK = 8
HY_CH_BATCH = 4
FILTER_ROWS = 256
WINDOW = 64
BF16_SUBLANES = 16
GATHER_TILES = 2
GATHER_SPLIT = 2
SAFE_SCORE_BOUND = 48.0
BF16_NORM_SLACK = 1.01
F32_MIN_EXP = -150.0
TOPK_EXP_STEPS = 10
TOPK_VALUE_STEPS = 28
ONES_ROWS = 16
Q_SCALE = HEAD_DIM ** -0.5 * math.log2(math.e)

NT_DIMS = (((1,), (1,)), ((), ()))


def _params(*sem):
    return pltpu.CompilerParams(dimension_semantics=sem, vmem_limit_bytes=VMEM_LIMIT)


def _split2(a):
    hi = a.astype(BF16)
    return hi, (a - hi.astype(F32)).astype(BF16)


def _dot_split_lhs(a, b_bf16):
    hi, lo = _split2(a)
    return (jnp.dot(hi, b_bf16, preferred_element_type=F32)
            + jnp.dot(lo, b_bf16, preferred_element_type=F32))


def _adaln_kernel(c_ref, w_ref, b_ref, o_ref):
    c = c_ref[...]
    s = c * jax.nn.sigmoid(c)
    o_ref[...] = jnp.dot(s, w_ref[...], precision=HIGHEST, preferred_element_type=F32) + b_ref[...]


def _adaln(c, w, b):
    bsz, d = c.shape
    n = w.shape[1]
    rows = 8
    cp = jnp.zeros((rows, d), F32).at[:bsz].set(c)
    tn = 1536
    out = pl.pallas_call(
        _adaln_kernel,
        grid=(n // tn,),
        in_specs=[pl.BlockSpec((rows, d), lambda j: (0, 0)),
                  pl.BlockSpec((d, tn), lambda j: (0, j)),
                  pl.BlockSpec((1, tn), lambda j: (0, j))],
        out_specs=pl.BlockSpec((rows, tn), lambda j: (0, j)),
        out_shape=jax.ShapeDtypeStruct((rows, n), F32),
        compiler_params=_params("parallel"),
        name="adaln",
    )(cp, w, b.reshape(1, n))
    return out[:bsz]


def _inproj_kernel(x_ref, mod_ref, g_ref, whyT_ref, bhy_ref, wr_ref, br_ref, cos_ref, sin_ref,
                   qg_ref, kg_ref, bd_ref, zhyT_ref, q_ref, k_ref, vt_ref, gate_ref):
    x = x_ref[0]
    ms = jnp.mean(x * x, axis=-1, keepdims=True)
    sh1 = mod_ref[0, 0:1, :]
    sc1 = mod_ref[0, 1:2, :]
    h = (x * lax.rsqrt(ms + EPS)) * g_ref[...] * (1.0 + sc1) + sh1
    hb = h.astype(BF16)
    zr = jnp.dot(hb, wr_ref[...], preferred_element_type=F32) + br_ref[...]

    cos = cos_ref[...]
    sin = sin_ref[...]
    bd = bd_ref[...]
    lane = lax.broadcasted_iota(I32, cos.shape, 1)
    first = (lane & (HEAD_DIM // 2 - 1)) < HEAD_DIM // 4

    def norm_rope(u, gain, scale):
        ss = _dot_split_lhs(u * u, bd)
        un = (u * lax.rsqrt(ss * (1.0 / HEAD_DIM) + EPS)) * gain
        rot = jnp.where(first, pltpu.roll(un, LANES - HEAD_DIM // 4, 1), pltpu.roll(un, HEAD_DIM // 4, 1))
        return (un * cos + rot * sin) * scale

    per_group = LANES // HEAD_DIM
    for j in range(ATT_WIDTH // LANES):
        qn = norm_rope(zr[:, j * LANES:(j + 1) * LANES], qg_ref[...], Q_SCALE).astype(BF16)
        for hl in range(per_group):
            q_ref[0, j * per_group + hl] = qn[:, hl * HEAD_DIM:(hl + 1) * HEAD_DIM]
    kn = norm_rope(zr[:, ATT_WIDTH:ATT_WIDTH + KV_WIDTH], kg_ref[...], 1.0).astype(BF16)
    for hl in range(N_KV_HEADS):
        k_ref[0, hl] = kn[:, hl * HEAD_DIM:(hl + 1) * HEAD_DIM]
    vt = zr[:, ATT_WIDTH + KV_WIDTH:ATT_WIDTH + 2 * KV_WIDTH].T.astype(BF16)
    for hl in range(N_KV_HEADS):
        vt_ref[0, hl, :HEAD_DIM, :] = vt[hl * HEAD_DIM:(hl + 1) * HEAD_DIM]
        vt_ref[0, hl, HEAD_DIM:, :] = jnp.ones((ONES_ROWS, vt.shape[1]), BF16)
    gate_ref[0] = jax.nn.sigmoid(zr[:, ATT_WIDTH + 2 * KV_WIDTH:])
    zhyT_ref[0] = lax.dot_general(whyT_ref[...], hb, NT_DIMS, preferred_element_type=F32) + bhy_ref[...]


def _rope_tables(seq):
    rows = seq // GRID_W
    t = np.arange(seq)
    row = (t // GRID_W).astype(np.float32)
    col = (t % GRID_W).astype(np.float32)
    half = HEAD_DIM // 2
    quarter = half // 2
    inv = (ROPE_THETA ** (-np.arange(0, half, 2, dtype=np.float32) / half)).astype(np.float32)
    ang_r = row[:, None] * inv[None, :]
    ang_c = col[:, None] * inv[None, :]
    ang = np.concatenate([ang_r, ang_r, ang_c, ang_c], axis=1)
    sign = np.concatenate([-np.ones(quarter), np.ones(quarter)] * 2).astype(np.float32)
    cos = np.cos(ang).astype(np.float32)
    sin = (np.sin(ang) * sign[None, :]).astype(np.float32)
    reps = LANES // HEAD_DIM
    del rows
    return jnp.asarray(np.tile(cos, (1, reps))), jnp.asarray(np.tile(sin, (1, reps)))


def _inproj(x, mod3, g_mix, w_in, b_in, q_gain, k_gain):
    bsz, seq, d = x.shape
    tm = INPROJ_TILE
    n_rest = w_in.shape[1] - HY_IN
    n_gate = n_rest - ATT_WIDTH - 2 * KV_WIDTH
    whyT = w_in[:, :HY_IN].T.astype(BF16)
    wr = w_in[:, HY_IN:].astype(BF16)
    bhy = b_in[:HY_IN].reshape(HY_IN, 1)
    br = b_in[HY_IN:].reshape(1, n_rest)
    cos, sin = _rope_tables(seq)
    reps = LANES // HEAD_DIM
    qg = jnp.tile(q_gain, reps).reshape(1, LANES)
    kg = jnp.tile(k_gain, reps).reshape(1, LANES)
    bd = jnp.asarray(np.kron(np.eye(reps, dtype=np.float32), np.ones((HEAD_DIM, HEAD_DIM), np.float32))).astype(BF16)
    full = lambda shape: pl.BlockSpec(shape, lambda b, i: (0,) * len(shape))
    return pl.pallas_call(
        _inproj_kernel,
        grid=(bsz, seq // tm),
        in_specs=[pl.BlockSpec((1, tm, d), lambda b, i: (b, i, 0)),
                  pl.BlockSpec((1, 6, d), lambda b, i: (b, 0, 0)),
                  full((1, d)), full((HY_IN, d)), full((HY_IN, 1)), full((d, n_rest)), full((1, n_rest)),
                  pl.BlockSpec((tm, LANES), lambda b, i: (i, 0)),
                  pl.BlockSpec((tm, LANES), lambda b, i: (i, 0)),
                  full((1, LANES)), full((1, LANES)), full((LANES, LANES))],
        out_specs=[pl.BlockSpec((1, HY_IN, tm), lambda b, i: (b, 0, i)),
                   pl.BlockSpec((1, N_HEADS, tm, HEAD_DIM), lambda b, i: (b, 0, i, 0)),
                   pl.BlockSpec((1, N_KV_HEADS, tm, HEAD_DIM), lambda b, i: (b, 0, i, 0)),
                   pl.BlockSpec((1, N_KV_HEADS, HEAD_DIM + ONES_ROWS, tm), lambda b, i: (b, 0, 0, i)),
                   pl.BlockSpec((1, tm, n_gate), lambda b, i: (b, i, 0))],
        out_shape=[jax.ShapeDtypeStruct((bsz, HY_IN, seq), F32),
                   jax.ShapeDtypeStruct((bsz, N_HEADS, seq, HEAD_DIM), BF16),
                   jax.ShapeDtypeStruct((bsz, N_KV_HEADS, seq, HEAD_DIM), BF16),
                   jax.ShapeDtypeStruct((bsz, N_KV_HEADS, HEAD_DIM + ONES_ROWS, seq), BF16),
                   jax.ShapeDtypeStruct((bsz, seq, n_gate), F32)],
        compiler_params=_params("parallel", "parallel"),
        name="inproj",
    )(x, mod3, g_mix.reshape(1, d), whyT, bhy, wr, br, cos, sin, qg, kg, bd)


def _filter_kernel(zT_ref, w1T_ref, b1_ref, w2T_ref, b2_ref, w3T_ref, b3_ref, fr_ref, w4T_ref,
                   t01_ref, absd_ref, o_ref, hdn_ref):
    @pl.when(pl.program_id(0) == 0)
    def _():
        fr = fr_ref[...]
        h = jnp.sin(fr[:, 0:1] * (jnp.dot(w1T_ref[...], zT_ref[...], precision=HIGHEST,
                                          preferred_element_type=F32) + b1_ref[...]))
        h = jnp.sin(fr[:, 1:2] * (jnp.dot(w2T_ref[...], h, precision=HIGHEST,
                                          preferred_element_type=F32) + b2_ref[...]))
        h = jnp.sin(fr[:, 2:3] * (jnp.dot(w3T_ref[...], h, precision=HIGHEST,
                                          preferred_element_type=F32) + b3_ref[...]))
        hdn_ref[...] = h

    w_hi, w_lo = _split2(w4T_ref[...])
    h_hi, h_lo = _split2(hdn_ref[...])
    hT = (jnp.dot(w_hi, h_hi, preferred_element_type=F32) + jnp.dot(w_lo, h_hi, preferred_element_type=F32)
          + jnp.dot(w_hi, h_lo, preferred_element_type=F32))
    hT = hT * jnp.exp(-t01_ref[...] * absd_ref[...])
    nrm = jnp.sum(jnp.abs(hT), axis=1, keepdims=True) + EPS
    o_ref[...] = hT * (1.0 / nrm)


def _filters(seq, w1, b1, w2, b2, w3, b3, w4, freq):
    emb_pad = 64
    bands = (FILTER_EMB - 1) // 2
    t01 = np.linspace(0.0, 1.0, seq, dtype=np.float32)[None, :]
    f = np.linspace(1e-4, bands - 1, bands, dtype=np.float32)[:, None]
    w = ((2.0 * math.pi) * np.arange(seq, dtype=np.float32) / seq).astype(np.float32)[None, :]
    zT = np.zeros((emb_pad, seq), np.float32)
    zT[0:1] = t01
    zT[1:1 + bands] = np.cos(f * w)
    zT[1 + bands:1 + 2 * bands] = -np.sin(f * w)
    w1T = jnp.zeros((FILTER_HIDDEN, emb_pad), F32).at[:, :FILTER_EMB].set(w1.T)
    max_decay = math.log(HY_DECAY_TARGET) / HY_QUICK_DECAY_PCT
    min_decay = math.log(HY_DECAY_TARGET) / HY_GRADUAL_DECAY_PCT
    deltas = np.abs(np.linspace(min_decay, max_decay, HY_WIDTH, dtype=np.float32))
    n_rows = HY_ORDER * 2 * HY_WIDTH
    absd = np.tile(deltas, HY_ORDER * 2).reshape(n_rows, 1)
    rb = FILTER_ROWS
    full = lambda shape: pl.BlockSpec(shape, lambda j: (0,) * len(shape))
    col = lambda v: v.reshape(FILTER_HIDDEN, 1)
    return pl.pallas_call(
        _filter_kernel,
        grid=(n_rows // rb,),
        in_specs=[full((emb_pad, seq)),
                  full((FILTER_HIDDEN, emb_pad)), full((FILTER_HIDDEN, 1)),
                  full((FILTER_HIDDEN, FILTER_HIDDEN)), full((FILTER_HIDDEN, 1)),
                  full((FILTER_HIDDEN, FILTER_HIDDEN)), full((FILTER_HIDDEN, 1)),
                  full((FILTER_HIDDEN, 3)),
                  pl.BlockSpec((rb, FILTER_HIDDEN), lambda j: (j, 0)),
                  full((1, seq)),
                  pl.BlockSpec((rb, 1), lambda j: (j, 0))],
        out_specs=pl.BlockSpec((rb, seq), lambda j: (j, 0)),
        out_shape=jax.ShapeDtypeStruct((n_rows, seq), F32),
        scratch_shapes=[pltpu.VMEM((FILTER_HIDDEN, seq), F32)],
        compiler_params=_params("arbitrary"),
        name="filters",
    )(jnp.asarray(zT), w1T, col(b1), w2.T, col(b2), w3.T, col(b3), freq.T, w4.T,
      jnp.asarray(t01), jnp.asarray(absd))


def _dft_tables(seq):
    n = 2 * seq
    r = int(round(math.sqrt(n)))
    assert r * r == n, "2*seq must be a perfect square"
    hh = r // 2
    rh = -(-(hh + 1) // BF16_SUBLANES) * BF16_SUBLANES
    k = np.arange(r, dtype=np.float64)
    ang = 2.0 * np.pi * np.outer(k, k) / r
    fr, fi = np.cos(ang), -np.sin(ang)
    angt = 2.0 * np.pi * np.outer(k, k) / n
    keep = np.zeros((rh, 1))
    keep[:hh + 1] = 1.0
    pad = lambda a: np.concatenate([a[:hh + 1], np.zeros((rh - hh - 1,) + a.shape[1:])], axis=0)
    twr, twi = pad(np.cos(angt)), pad(-np.sin(angt))
    f1 = np.concatenate([pad(fr[:, :hh]), pad(fi[:, :hh])], axis=0)
    f3 = np.block([[fr, fi], [-fi, fr]])
    f3i = np.block([[fr, -fi], [fi, fr]])
    weight = np.full((rh, 1), 2.0) * keep
    weight[0] = weight[hh] = 1.0
    f1i = np.concatenate([(pad(fr[:, :hh]) * weight).T, (pad(fi[:, :hh]) * weight).T], axis=1) / n
    as32 = lambda a: jnp.asarray(a.astype(np.float32))
    return as32(f1), as32(twr), as32(twi), as32(f3), as32(f3i), as32(f1i)


def _hyena_kernel(sw_ref, sb_ref, fb_ref, v_ref, x1_ref, x2_ref, h_ref, f1_f32, twr_ref, twi_ref,
                  f3_f32, f3i_f32, f1i_f32, o_ref, f1_ref, f3_ref, f3i_ref, f1i_ref):
    bsz, cb, hh, r = v_ref.shape
    rh = twr_ref.shape[0]
    j = pl.program_id(0)
    f1_ref[...] = f1_f32[...].astype(BF16)
    f3_ref[...] = f3_f32[...].astype(BF16)
    f3i_ref[...] = f3i_f32[...].astype(BF16)
    f1i_ref[...] = f1i_f32[...].astype(BF16)
    twr = twr_ref[...]
    twi = twi_ref[...]
    lane = lax.broadcasted_iota(I32, (hh, r), 1)
    row = lax.broadcasted_iota(I32, (hh, r), 0)

    def shift_prev(z):
        a = pltpu.roll(z, 1, 1)
        a = jnp.where(lane == 0, pltpu.roll(a, 1, 0), a)
        return jnp.where((lane == 0) & (row == 0), 0.0, a)

    def shift_next(z):
        a = pltpu.roll(z, r - 1, 1)
        a = jnp.where(lane == r - 1, pltpu.roll(a, hh - 1, 0), a)
        return jnp.where((lane == r - 1) & (row == hh - 1), 0.0, a)

    def fwd(slabs):
        m = len(slabs)
        acat = jnp.concatenate([a.astype(BF16) for a in slabs], axis=1)
        s = jnp.dot(f1_ref[...], acat, preferred_element_type=F32)
        lhs = []
        for i in range(m):
            br = s[:rh, i * r:(i + 1) * r]
            bi = s[rh:, i * r:(i + 1) * r]
            cr = br * twr - bi * twi
            ci = br * twi + bi * twr
            lhs.append(jnp.concatenate([cr, ci], axis=1).astype(BF16))
        d = jnp.dot(jnp.concatenate(lhs, axis=0), f3_ref[...], preferred_element_type=F32)
        return [(d[i * rh:(i + 1) * rh, :r], d[i * rh:(i + 1) * rh, r:]) for i in range(m)]

    def inv(specs):
        m = len(specs)
        lhs = jnp.concatenate([jnp.concatenate([pr, pi], axis=1).astype(BF16) for pr, pi in specs], axis=0)
        e = jnp.dot(lhs, f3i_ref[...], preferred_element_type=F32)
        rhs = []
        for i in range(m):
            er = e[i * rh:(i + 1) * rh, :r]
            ei = e[i * rh:(i + 1) * rh, r:]
            tr = er * twr + ei * twi
            ti = ei * twr - er * twi
            rhs.append(jnp.concatenate([tr, ti], axis=0).astype(BF16))
        y = jnp.dot(f1i_ref[...], jnp.concatenate(rhs, axis=1), preferred_element_type=F32)
        return [y[:, i * r:(i + 1) * r] for i in range(m)]

    def cmul(a, g):
        return a[0] * g[0] - a[1] * g[1], a[0] * g[1] + a[1] * g[0]

    def sconv(ref, b, c, gi):
        z = ref[b, c]
        return (sb_ref[gi] + sw_ref[gi] * shift_prev(z) + sw_ref[HY_IN + gi] * z
                + sw_ref[2 * HY_IN + gi] * shift_next(z))

    nb = HY_CH_BATCH

    def body(t, carry):
        cs = [t * nb + i for i in range(nb)]
        chs = [j * cb + c for c in cs]
        pairs = [(c, ch, b) for c, ch in zip(cs, chs) for b in range(bsz)]
        v = [sconv(v_ref, b, c, ch) for c, ch, b in pairs]
        filt = [h_ref[o, d, c] for c in cs for o in range(HY_ORDER) for d in range(2)]
        spec = fwd(v + filt)
        nv = len(v)
        g = []
        for i in range(nb):
            gi = []
            for o in range(HY_ORDER):
                hf = spec[nv + (i * HY_ORDER + o) * 2]
                hb = spec[nv + (i * HY_ORDER + o) * 2 + 1]
                gi.append((hf[0] + hb[0], hf[1] - hb[1]))
            g.append(gi)
        conv = inv([cmul(spec[n], g[n // bsz][0]) for n in range(nv)])
        y = [sconv(x1_ref, b, c, HY_WIDTH + ch) * (conv[n] + fb_ref[ch] * v[n])
             for n, (c, ch, b) in enumerate(pairs)]
        spec = fwd(y)
        conv = inv([cmul(spec[n], g[n // bsz][1]) for n in range(nv)])
        for n, (c, ch, b) in enumerate(pairs):
            o_ref[b, c] = (sconv(x2_ref, b, c, 2 * HY_WIDTH + ch)
                           * (conv[n] + fb_ref[HY_WIDTH + ch] * y[n]))
        return carry

    lax.fori_loop(0, cb // nb, body, 0)


def _hyena(zhyT, hT, short_w, short_b, filt_bias):
    bsz, _, seq = zhyT.shape
    tabs = _dft_tables(seq)
    r = tabs[1].shape[1]
    hh = r // 2
    z4 = zhyT.reshape(bsz, HY_IN, hh, r)
    h5 = hT.reshape(HY_ORDER, 2, HY_WIDTH, hh, r)
    cb = HY_CH_BLOCK
    nblk = HY_WIDTH // cb
    smem = pl.BlockSpec(memory_space=pltpu.SMEM)
    full = lambda a: pl.BlockSpec(a.shape, lambda j: (0,) * a.ndim)
    zspec = lambda off: pl.BlockSpec((bsz, cb, hh, r), lambda j: (0, off * nblk + j, 0, 0))
    out = pl.pallas_call(
        _hyena_kernel,
        grid=(nblk,),
        in_specs=[smem, smem, smem, zspec(0), zspec(1), zspec(2),
                  pl.BlockSpec((HY_ORDER, 2, cb, hh, r), lambda j: (0, 0, j, 0, 0))]
                 + [full(t) for t in tabs],
        out_specs=pl.BlockSpec((bsz, cb, hh, r), lambda j: (0, j, 0, 0)),
        out_shape=jax.ShapeDtypeStruct((bsz, HY_WIDTH, hh, r), F32),
        scratch_shapes=[pltpu.VMEM(tabs[i].shape, BF16) for i in (0, 3, 4, 5)],
        compiler_params=_params("parallel"),
        name="hyena",
    )(short_w.reshape(-1), short_b, filt_bias.reshape(-1), z4, z4, z4, h5, *tabs)
    return out.reshape(bsz, HY_WIDTH, seq)


def _attn_kernel(bounded_ref, q_ref, k_ref, vt_ref, o_ref):
    g, tq, hd = q_ref.shape[1:]
    seq = k_ref.shape[2]
    rows = vt_ref.shape[2]
    n_chunks = seq // KV_CHUNK
    q = q_ref[0].reshape(g * tq, hd)

    def scores(kc):
        kb = k_ref[0, 0, kc * KV_CHUNK:(kc + 1) * KV_CHUNK, :]
        return lax.dot_general(kb, q, NT_DIMS, preferred_element_type=F32)

    def values(kc):
        return vt_ref[0, 0, :, kc * KV_CHUNK:(kc + 1) * KV_CHUNK]

    def finish(acc):
        out = acc[:hd] / acc[hd:hd + 1]
        o_ref[0] = out.T.reshape(g, tq, hd).astype(o_ref.dtype)

    @pl.when(bounded_ref[0] == 1)
    def _():
        acc = jnp.zeros((rows, g * tq), F32)
        for kc in range(n_chunks):
            p = jnp.exp2(scores(kc)).astype(BF16)
            acc = acc + jnp.dot(values(kc), p, preferred_element_type=F32)
        finish(acc)

    @pl.when(bounded_ref[0] != 1)
    def _():
        m = jnp.full((1, g * tq), -jnp.inf, F32)
        acc = jnp.zeros((rows, g * tq), F32)
        s_next = scores(0)
        for kc in range(n_chunks):
            s = s_next
            if kc + 1 < n_chunks:
                s_next = scores(kc + 1)
            m_new = jnp.maximum(m, jnp.max(s, axis=0, keepdims=True))
            p = jnp.exp2(s - m_new).astype(BF16)
            acc = jnp.exp2(m - m_new) * acc + jnp.dot(values(kc), p, preferred_element_type=F32)
            m = m_new
        finish(acc)


def _attention(qh, kh, vt, q_gain, k_gain):
    bsz, _, seq, _ = qh.shape
    g = N_HEADS // N_KV_HEADS
    rows = vt.shape[2]
    bound = HEAD_DIM * Q_SCALE * BF16_NORM_SLACK * jnp.max(jnp.abs(q_gain)) * jnp.max(jnp.abs(k_gain))
    bounded = (bound <= SAFE_SCORE_BOUND).astype(I32).reshape(1)
    tq = Q_TILE
    grid_spec = pltpu.PrefetchScalarGridSpec(
        num_scalar_prefetch=1,
        grid=(bsz, N_KV_HEADS, seq // tq),
        in_specs=[pl.BlockSpec((1, g, tq, HEAD_DIM), lambda b, h, i, f: (b, h, i, 0)),
                  pl.BlockSpec((1, 1, seq, HEAD_DIM), lambda b, h, i, f: (b, h, 0, 0)),
                  pl.BlockSpec((1, 1, rows, seq), lambda b, h, i, f: (b, h, 0, 0))],
        out_specs=pl.BlockSpec((1, g, tq, HEAD_DIM), lambda b, h, i, f: (b, h, i, 0)))
    return pl.pallas_call(
        _attn_kernel,
        grid_spec=grid_spec,
        out_shape=jax.ShapeDtypeStruct((bsz, N_HEADS, seq, HEAD_DIM), BF16),
        compiler_params=_params("parallel", "parallel", "parallel"),
        name="attention",
    )(bounded, qh, kh, vt)


def _merge_kernel(x_ref, yhyT_ref, yat_ref, gate_ref, mod_ref, g_ref, why_ref, wat_ref, wout_ref, wr_ref,
                  x1_ref, h2_ref, aff_ref):
    d = x_ref.shape[2]
    yhy = yhyT_ref[0].T.astype(BF16)
    a = jnp.dot(yhy, why_ref[...], preferred_element_type=F32)
    yat = jnp.concatenate([yat_ref[0, h] for h in range(yat_ref.shape[1])], axis=1)
    b = jnp.dot(yat, wat_ref[...], preferred_element_type=F32)
    gate = gate_ref[0]
    mrg = gate[:, :d] * a + gate[:, d:] * b
    o = jnp.dot(mrg.astype(BF16), wout_ref[...], preferred_element_type=F32)
    gt1 = mod_ref[0, 2:3, :]
    sh2 = mod_ref[0, 3:4, :]
    sc2 = mod_ref[0, 4:5, :]
    x1 = x_ref[0] + gt1 * o
    x1_ref[0] = x1
    ms = jnp.mean(x1 * x1, axis=-1, keepdims=True)
    h2 = (x1 * lax.rsqrt(ms + EPS)) * g_ref[...] * (1.0 + sc2) + sh2
    hi, lo = _split2(h2)
    h2_ref[0] = hi
    w_hi, w_lo = _split2(wr_ref[...])
    logits = (jnp.dot(hi, w_hi, preferred_element_type=F32)
              + jnp.dot(lo, w_hi, preferred_element_type=F32)
              + jnp.dot(hi, w_lo, preferred_element_type=F32))
    ex = jnp.exp(logits - jnp.max(logits, axis=-1, keepdims=True))
    aff_ref[0] = ex / jnp.sum(ex, axis=-1, keepdims=True)


def _merge(x, yhyT, yat, gates, mod3, g_ffn, w_hy_out, w_att_out, w_out, w_router):
    bsz, seq, d = x.shape
    tm = TOKEN_TILE
    ne = w_router.shape[1]
    full = lambda shape: pl.BlockSpec(shape, lambda b, i: (0,) * len(shape))
    return pl.pallas_call(
        _merge_kernel,
        grid=(bsz, seq // tm),
        in_specs=[pl.BlockSpec((1, tm, d), lambda b, i: (b, i, 0)),
                  pl.BlockSpec((1, HY_WIDTH, tm), lambda b, i: (b, 0, i)),
                  pl.BlockSpec((1, N_HEADS, tm, HEAD_DIM), lambda b, i: (b, 0, i, 0)),
                  pl.BlockSpec((1, tm, 2 * d), lambda b, i: (b, i, 0)),
                  pl.BlockSpec((1, 6, d), lambda b, i: (b, 0, 0)),
                  full((1, d)), full((HY_WIDTH, d)), full((ATT_WIDTH, d)), full((d, d)), full((d, ne))],
        out_specs=[pl.BlockSpec((1, tm, d), lambda b, i: (b, i, 0)),
                   pl.BlockSpec((1, tm, d), lambda b, i: (b, i, 0)),
                   pl.BlockSpec((1, tm, ne), lambda b, i: (b, i, 0))],
        out_shape=[jax.ShapeDtypeStruct((bsz, seq, d), F32),
                   jax.ShapeDtypeStruct((bsz, seq, d), BF16),
                   jax.ShapeDtypeStruct((bsz, seq, ne), F32)],
        compiler_params=_params("parallel", "parallel"),
        name="merge",
    )(x, yhyT, yat, gates, mod3, g_ffn.reshape(1, d), w_hy_out.astype(BF16), w_att_out.astype(BF16),
      w_out.astype(BF16), w_router)


def _topk_kernel(a_ref, ut_ref, pos_ref, *, cap):
    nr, seq = a_ref.shape
    a = a_ref[...]

    def count(mask):
        return jnp.sum(mask.astype(F32), axis=1, keepdims=True)

    def probe(t, lo, hi):
        ok = count(a >= t) >= cap
        return ok, jnp.where(ok, t, lo), jnp.where(ok, hi, t)

    def by_exponent(_, c):
        elo, ehi, lo, hi = c
        emid = 0.5 * (elo + ehi)
        ok, lo, hi = probe(jnp.exp2(emid), lo, hi)
        return jnp.where(ok, emid, elo), jnp.where(ok, ehi, emid), lo, hi

    def by_value(_, c):
        lo, hi = c
        _, lo, hi = probe(0.5 * (lo + hi), lo, hi)
        return lo, hi

    col = lambda v: jnp.full((nr, 1), v, F32)
    _, _, lo, hi = lax.fori_loop(0, TOPK_EXP_STEPS, by_exponent, (col(F32_MIN_EXP), col(1.0), col(0.0), col(2.0)))
    lo, hi = lax.fori_loop(0, TOPK_VALUE_STEPS, by_value, (lo, hi))
    gt = a >= hi
    eq = (a >= lo) & (a < hi)
    need = cap - count(gt)
    ut = ut_ref[...]
    run_eq = jnp.zeros((nr, 1), F32)
    run_sel = jnp.zeros((nr, 1), F32)
    for ch in range(seq // LANES):
        sl = slice(ch * LANES, (ch + 1) * LANES)
        eq_rank = jnp.dot(eq[:, sl].astype(BF16), ut, preferred_element_type=F32) + run_eq
        run_eq = eq_rank[:, LANES - 1:LANES]
        sel = gt[:, sl] | (eq[:, sl] & (eq_rank <= need))
        pos = jnp.dot(sel.astype(BF16), ut, preferred_element_type=F32) + run_sel
        run_sel = pos[:, LANES - 1:LANES]
        pos_ref[:, sl] = jnp.where(sel, pos.astype(I32) - 1, -1)


def _topk(aff2, cap):
    nr, seq = aff2.shape
    ut = jnp.asarray(np.triu(np.ones((LANES, LANES), np.float32))).astype(BF16)
    return pl.pallas_call(
        functools.partial(_topk_kernel, cap=cap),
        grid=(1,),
        in_specs=[pl.BlockSpec((nr, seq), lambda i: (0, 0)), pl.BlockSpec((LANES, LANES), lambda i: (0, 0))],
        out_specs=pl.BlockSpec((nr, seq), lambda i: (0, 0)),
        out_shape=jax.ShapeDtypeStruct((nr, seq), I32),
        compiler_params=_params("arbitrary"),
        name="topk",
    )(aff2, ut)


def _window_start(st_ref, idx, w, cap):
    base = st_ref[idx] + w * WINDOW
    return base, pl.multiple_of(jnp.minimum(base, cap - WINDOW), BF16_SUBLANES)


def _gather_kernel(st_ref, np_ref, pos_ref, h2_ref, xe_ref):
    eh = pos_ref.shape[1]
    tile = TOKEN_TILE
    per_step = pos_ref.shape[2] // tile
    cap = xe_ref.shape[2]
    steps = pl.num_programs(2)
    nt = steps * per_step
    b, half, i = pl.program_id(0), pl.program_id(1), pl.program_id(2)
    ne = pl.num_programs(1) * eh

    @pl.when(i == 0)
    def _():
        xe_ref[...] = jnp.zeros_like(xe_ref)

    rho = lax.broadcasted_iota(I32, (WINDOW, tile), 0)

    def one_pass(w, carry):
        starts, got = [], []
        for t in range(per_step):
            tok = slice(t * tile, (t + 1) * tile)
            rows = []
            for el in range(eh):
                base, st = _window_start(st_ref, (b * ne + half * eh + el) * nt + i * per_step + t, w, cap)
                pos = pos_ref[0, el:el + 1, tok]
                rows.append(((pos - st == rho) & (pos >= base)).astype(BF16))
                starts.append(st)
            got.append(jnp.dot(jnp.concatenate(rows, axis=0), h2_ref[0, tok, :],
                               preferred_element_type=F32).astype(BF16))
        for t in range(per_step):
            for el in range(eh):
                xe_ref[0, el, pl.ds(starts[t * eh + el], WINDOW), :] += got[t][el * WINDOW:(el + 1) * WINDOW]
        return carry

    lax.fori_loop(0, np_ref[(b * pl.num_programs(1) + half) * steps + i], one_pass, 0)


def _gather(st16, npass_gather, pos, h2, cap):
    bsz, ne, seq = pos.shape
    d = h2.shape[2]
    tile = GATHER_TILES * TOKEN_TILE
    eh = ne // GATHER_SPLIT
    grid_spec = pltpu.PrefetchScalarGridSpec(
        num_scalar_prefetch=2,
        grid=(bsz, GATHER_SPLIT, seq // tile),
        in_specs=[pl.BlockSpec((1, eh, tile), lambda b, h, i, st, n: (b, h, i)),
                  pl.BlockSpec((1, tile, d), lambda b, h, i, st, n: (b, i, 0))],
        out_specs=pl.BlockSpec((1, eh, cap, d), lambda b, h, i, st, n: (b, h, 0, 0)))
    return pl.pallas_call(
        _gather_kernel,
        grid_spec=grid_spec,
        out_shape=jax.ShapeDtypeStruct((bsz, ne, cap, d), BF16),
        compiler_params=_params("parallel", "parallel", "arbitrary"),
        name="gather",
    )(st16, npass_gather, pos, h2)


def _moe_kernel(xe_ref, wg_ref, wu_ref, wd_ref, ye_ref, wg_s, wu_s, wd_s):
    tile = TOKEN_TILE
    k = pl.program_id(0)
    b = pl.program_id(1)
    ne = pl.num_programs(0) - 1
    rows_in = wg_ref.shape[1]
    rows_ff = wd_ref.shape[1]

    @pl.when(k < ne)
    def _():
        nxt = k % 2
        wg_s[nxt, pl.ds(pl.multiple_of(b * rows_in, rows_in), rows_in), :] = wg_ref[0].astype(BF16)
        wu_s[nxt, pl.ds(pl.multiple_of(b * rows_in, rows_in), rows_in), :] = wu_ref[0].astype(BF16)
        wd_s[nxt, pl.ds(pl.multiple_of(b * rows_ff, rows_ff), rows_ff), :] = wd_ref[0].astype(BF16)

    @pl.when(k == 0)
    def _():
        ye_ref[...] = jnp.zeros_like(ye_ref)

    @pl.when(k > 0)
    def _():
        cur = (k + 1) % 2
        for r in range(xe_ref.shape[2] // tile):
            xe = xe_ref[0, 0, r * tile:(r + 1) * tile, :]
            a = jnp.dot(xe, wg_s[cur], preferred_element_type=F32)
            u = jnp.dot(xe, wu_s[cur], preferred_element_type=F32)
            hmid = (a * jax.nn.sigmoid(a) * u).astype(BF16)
            ye_ref[0, 0, r * tile:(r + 1) * tile, :] = jnp.dot(hmid, wd_s[cur],
                                                              preferred_element_type=F32).astype(BF16)


def _moe(xe, wg, wu, wd):
    bsz, ne, cap, d = xe.shape
    dff = wg.shape[2]
    assert d % bsz == 0 and dff % bsz == 0
    prev = lambda k: jnp.maximum(k - 1, 0)
    this = lambda k: jnp.minimum(k, ne - 1)
    return pl.pallas_call(
        _moe_kernel,
        grid=(ne + 1, bsz),
        in_specs=[pl.BlockSpec((1, 1, cap, d), lambda k, b: (b, prev(k), 0, 0)),
                  pl.BlockSpec((1, d // bsz, dff), lambda k, b: (this(k), b, 0)),
                  pl.BlockSpec((1, d // bsz, dff), lambda k, b: (this(k), b, 0)),
                  pl.BlockSpec((1, dff // bsz, d), lambda k, b: (this(k), b, 0))],
        out_specs=pl.BlockSpec((1, 1, cap, d), lambda k, b: (b, jnp.where(k == 0, ne, k - 1), 0, 0)),
        out_shape=jax.ShapeDtypeStruct((bsz, ne + 1, cap, d), BF16),
        scratch_shapes=[pltpu.VMEM((2, d, dff), BF16), pltpu.VMEM((2, d, dff), BF16),
                        pltpu.VMEM((2, dff, d), BF16)],
        compiler_params=_params("arbitrary", "arbitrary"),
        name="moe",
    )(xe, wg, wu, wd)


def _combine_kernel(st_ref, np_ref, pos_ref, aff_ref, ye_ref, x1_ref, mod_ref, ex_ref, rho_ref, o_ref,
                    rhs_ref, acc_ref):
    tile, ne = pos_ref.shape[1:]
    cap = ye_ref.shape[2]
    nt = pl.num_programs(1)
    b, i = pl.program_id(0), pl.program_id(1)
    acc_ref[...] = jnp.zeros_like(acc_ref)
    lane = lax.broadcasted_iota(I32, (1, ne), 1)
    pos = pos_ref[0]
    aexp = jnp.dot(aff_ref[0].astype(BF16), ex_ref[...], preferred_element_type=F32)

    def one_pass(w, carry):
        base_v = jnp.zeros((1, ne), I32)
        st_v = jnp.zeros((1, ne), I32)
        for e in range(ne):
            base, st = _window_start(st_ref, (b * ne + e) * nt + i, w, cap)
            rhs_ref[e * WINDOW:(e + 1) * WINDOW, :] = ye_ref[0, e, pl.ds(st, WINDOW), :]
            base_v = jnp.where(lane == e, base, base_v)
            st_v = jnp.where(lane == e, st, st_v)
        off = pos - st_v
        off = jnp.where((pos >= base_v) & (off >= 0) & (off < WINDOW), off, -1)
        oexp = jnp.dot(off.astype(F32).astype(BF16), ex_ref[...], preferred_element_type=F32)
        pt = jnp.where(oexp == rho_ref[...], aexp, 0.0).astype(BF16)
        acc_ref[...] += jnp.dot(pt, rhs_ref[...], preferred_element_type=F32)
        return carry

    lax.fori_loop(0, np_ref[b * nt + i], one_pass, 0)
    o_ref[0] = x1_ref[0] + mod_ref[0, 5:6, :] * acc_ref[...]


def _combine(st16, npass, pos_tok, aff_tok, ye, x1, mod3):
    bsz, seq, d = x1.shape
    ne = aff_tok.shape[2]
    cap = ye.shape[2]
    tile = TOKEN_TILE
    spread =np.kron(np.eye(ne, dtype=np.float32), np.ones((1, WINDOW), np.float32))
    rho = np.tile(np.arange(WINDOW, dtype=np.float32), ne).reshape(1, ne * WINDOW)
    grid_spec = pltpu.PrefetchScalarGridSpec(
        num_scalar_prefetch=2,
        grid=(bsz, seq // tile),
        in_specs=[pl.BlockSpec((1, tile, ne), lambda b, i, st, n: (b, i, 0)),
                  pl.BlockSpec((1, tile, ne), lambda b, i, st, n: (b, i, 0)),
                  pl.BlockSpec((1, ne, cap, d), lambda b, i, st, n: (b, 0, 0, 0), pipeline_mode=pl.Buffered(1)),
                  pl.BlockSpec((1, tile, d), lambda b, i, st, n: (b, i, 0)),
                  pl.BlockSpec((1, 6, d), lambda b, i, st, n: (b, 0, 0)),
                  pl.BlockSpec((ne, ne * WINDOW), lambda b, i, st, n: (0, 0)),
                  pl.BlockSpec((1, ne * WINDOW), lambda b, i, st, n: (0, 0))],
        out_specs=pl.BlockSpec((1, tile, d), lambda b, i, st, n: (b, i, 0)),
        scratch_shapes=[pltpu.VMEM((ne * WINDOW, d), BF16), pltpu.VMEM((tile, d), F32)])
    return pl.pallas_call(
        _combine_kernel,
        grid_spec=grid_spec,
        out_shape=jax.ShapeDtypeStruct((bsz, seq, d), F32),
        compiler_params=_params("parallel", "arbitrary"),
        name="combine",
    )(st16, npass, pos_tok, aff_tok, ye, x1, mod3, jnp.asarray(spread).astype(BF16), jnp.asarray(rho))


def _route_tables(pos):
    bsz, ne, seq = pos.shape
    tile = TOKEN_TILE
    counts = jnp.sum((pos >= 0).reshape(bsz, ne, seq // tile, tile), axis=-1, dtype=I32)
    starts = jnp.cumsum(counts, axis=-1) - counts
    st16 = (starts // BF16_SUBLANES) * BF16_SUBLANES
    span = jnp.where(counts > 0, starts - st16 + counts, 0)
    passes = (span + WINDOW - 1) // WINDOW
    npass = jnp.max(passes, axis=1)
    npass_gather = jnp.max(passes.reshape(bsz, GATHER_SPLIT, ne // GATHER_SPLIT, -1, GATHER_TILES), axis=(2, 4))
    return st16.reshape(-1), npass.reshape(-1), npass_gather.reshape(-1)


def kernel(x, c, w_ada, b_ada, g_mix, g_ffn, w_in, b_in, short_w, short_b, hy_w1, hy_b1, hy_w2, hy_b2, hy_w3, hy_b3, hy_w4, hy_freq, hy_bias, q_gain, k_gain, w_hy_out, w_att_out, w_out, w_router, w_gate, w_up, w_down):
    bsz, seq, d = x.shape
    depth = w_ada.shape[0]
    ne = w_router.shape[-1]
    cap = EC_FACTOR * seq // ne
    for l in range(depth):
        mod3 = _adaln(c, w_ada[l], b_ada[l]).reshape(bsz, 6, d)
        zhyT, q, k, v, gates = _inproj(x, mod3, g_mix[l], w_in[l], b_in[l], q_gain[l], k_gain[l])
        hT = _filters(seq, hy_w1[l], hy_b1[l], hy_w2[l], hy_b2[l], hy_w3[l], hy_b3[l], hy_w4[l], hy_freq[l])
        yhyT = _hyena(zhyT, hT, short_w[l], short_b[l], hy_bias[l])
        yat = _attention(q, k, v, q_gain[l], k_gain[l])
        x1, h2, aff = _merge(x, yhyT, yat, gates, mod3, g_ffn[l], w_hy_out[l], w_att_out[l], w_out[l],
                             w_router[l])
        pos = _topk(aff.transpose(0, 2, 1).reshape(bsz * ne, seq), cap).reshape(bsz, ne, seq)
        st16, npass, npass_gather = _route_tables(pos)
        xe = _gather(st16, npass_gather, pos, h2, cap)
        ye = _moe(xe, w_gate[l], w_up[l], w_down[l])
        x = _combine(st16, npass, pos.transpose(0, 2, 1), aff, ye, x1, mod3)
    return x
```

```python
import functools
import math

import numpy as np
import jax
import jax.numpy as jnp
from jax import lax
from jax.experimental import pallas as pl
from jax.experimental.pallas import tpu as pltpu

F32 = jnp.float32
BF16 = jnp.bfloat16
I32 = jnp.int32
HIGHEST = lax.Precision.HIGHEST

EPS = 1e-6
GRID_W = 64
HY_WIDTH = 512
HY_ORDER = 2
SHORT_K = 3
FILTER_EMB = 33
FILTER_HIDDEN = 64
HY_QUICK_DECAY_PCT = 0.3
HY_GRADUAL_DECAY_PCT = 1.5
HY_DECAY_TARGET = 1e-2
N_HEADS = 8
N_KV_HEADS = 2
HEAD_DIM = 64
ROPE_THETA = 10000.0
N_EXPERTS = 16
EC_FACTOR = 2
HY_IN = 3 * HY_WIDTH
ATT_WIDTH = N_HEADS * HEAD_DIM
KV_WIDTH = N_KV_HEADS * HEAD_DIM

LANES = 128
VMEM_LIMIT = 56 * 1024 * 1024
MXU_TILE = 256
TOKEN_TILE = 256
INPROJ_TILE = 512
Q_TILE = 256
KV_CHUNK = 2048
HY_CH_BLOCK = 8
HY_CH_BATCH = 4
FILTER_ROWS = 256
WINDOW = 64
BF16_SUBLANES = 16
COMBINE_TILES = 2
GATHER_TILES = 4
GATHER_SPLIT = 2
SAFE_SCORE_BOUND = 48.0
BF16_NORM_SLACK = 1.01
F32_MIN_EXP = -150.0
TOPK_EXP_STEPS = 10
TOPK_VALUE_STEPS = 28
ONES_ROWS = 16
Q_SCALE = HEAD_DIM ** -0.5 * math.log2(math.e)

NT_DIMS = (((1,), (1,)), ((), ()))


def _params(*sem):
    return pltpu.CompilerParams(dimension_semantics=sem, vmem_limit_bytes=VMEM_LIMIT)


def _split2(a):
    hi = a.astype(BF16)
    return hi, (a - hi.astype(F32)).astype(BF16)


def _dot_split_lhs(a, b_bf16):
    hi, lo = _split2(a)
    return (jnp.dot(hi, b_bf16, preferred_element_type=F32)
            + jnp.dot(lo, b_bf16, preferred_element_type=F32))


def _adaln_kernel(c_ref, w_ref, b_ref, o_ref):
    c = c_ref[...]
    s = c * jax.nn.sigmoid(c)
    o_ref[...] = jnp.dot(s, w_ref[...], precision=HIGHEST, preferred_element_type=F32) + b_ref[...]


def _adaln(c, w, b):
    bsz, d = c.shape
    n = w.shape[1]
    rows = 8
    cp = jnp.zeros((rows, d), F32).at[:bsz].set(c)
    tn = 1536
    out = pl.pallas_call(
        _adaln_kernel,
        grid=(n // tn,),
        in_specs=[pl.BlockSpec((rows, d), lambda j: (0, 0)),
                  pl.BlockSpec((d, tn), lambda j: (0, j)),
                  pl.BlockSpec((1, tn), lambda j: (0, j))],
        out_specs=pl.BlockSpec((rows, tn), lambda j: (0, j)),
        out_shape=jax.ShapeDtypeStruct((rows, n), F32),
        compiler_params=_params("parallel"),
        name="adaln",
    )(cp, w, b.reshape(1, n))
    return out[:bsz]


def _inproj_kernel(x_ref, mod_ref, g_ref, whyT_ref, bhy_ref, wr_ref, br_ref, cos_ref, sin_ref,
                   qg_ref, kg_ref, bd_ref, zhyT_ref, q_ref, k_ref, vt_ref, gate_ref):
    x = x_ref[0]
    ms = jnp.mean(x * x, axis=-1, keepdims=True)
    sh1 = mod_ref[0, 0:1, :]
    sc1 = mod_ref[0, 1:2, :]
    h = (x * lax.rsqrt(ms + EPS)) * g_ref[...] * (1.0 + sc1) + sh1
    hb = h.astype(BF16)
    zr = jnp.dot(hb, wr_ref[...], preferred_element_type=F32) + br_ref[...]

    cos = cos_ref[...]
    sin = sin_ref[...]
    bd = bd_ref[...]
    lane = lax.broadcasted_iota(I32, cos.shape, 1)
    first = (lane & (HEAD_DIM // 2 - 1)) < HEAD_DIM // 4

    def norm_rope(u, gain, scale):
        ss = _dot_split_lhs(u * u, bd)
        un = (u * lax.rsqrt(ss * (1.0 / HEAD_DIM) + EPS)) * gain
        rot = jnp.where(first, pltpu.roll(un, LANES - HEAD_DIM // 4, 1), pltpu.roll(un, HEAD_DIM // 4, 1))
        return (un * cos + rot * sin) * scale

    per_group = LANES // HEAD_DIM
    for j in range(ATT_WIDTH // LANES):
        qn = norm_rope(zr[:, j * LANES:(j + 1) * LANES], qg_ref[...], Q_SCALE).astype(BF16)
        for hl in range(per_group):
            q_ref[0, j * per_group + hl] = qn[:, hl * HEAD_DIM:(hl + 1) * HEAD_DIM]
    kn = norm_rope(zr[:, ATT_WIDTH:ATT_WIDTH + KV_WIDTH], kg_ref[...], 1.0).astype(BF16)
    for hl in range(N_KV_HEADS):
        k_ref[0, hl] = kn[:, hl * HEAD_DIM:(hl + 1) * HEAD_DIM]
    vt = zr[:, ATT_WIDTH + KV_WIDTH:ATT_WIDTH + 2 * KV_WIDTH].T.astype(BF16)
    for hl in range(N_KV_HEADS):
        vt_ref[0, hl, :HEAD_DIM, :] = vt[hl * HEAD_DIM:(hl + 1) * HEAD_DIM]
        vt_ref[0, hl, HEAD_DIM:, :] = jnp.ones((ONES_ROWS, vt.shape[1]), BF16)
    gate_ref[0] = jax.nn.sigmoid(zr[:, ATT_WIDTH + 2 * KV_WIDTH:])
    zhyT_ref[0] = lax.dot_general(whyT_ref[...], hb, NT_DIMS, preferred_element_type=F32) + bhy_ref[...]


def _rope_tables(seq):
    rows = seq // GRID_W
    t = np.arange(seq)
    row = (t // GRID_W).astype(np.float32)
    col = (t % GRID_W).astype(np.float32)
    half = HEAD_DIM // 2
    quarter = half // 2
    inv = (ROPE_THETA ** (-np.arange(0, half, 2, dtype=np.float32) / half)).astype(np.float32)
    ang_r = row[:, None] * inv[None, :]
    ang_c = col[:, None] * inv[None, :]
    ang = np.concatenate([ang_r, ang_r, ang_c, ang_c], axis=1)
    sign = np.concatenate([-np.ones(quarter), np.ones(quarter)] * 2).astype(np.float32)
    cos = np.cos(ang).astype(np.float32)
    sin = (np.sin(ang) * sign[None, :]).astype(np.float32)
    reps = LANES // HEAD_DIM
    del rows
    return jnp.asarray(np.tile(cos, (1, reps))), jnp.asarray(np.tile(sin, (1, reps)))


def _inproj(x, mod3, g_mix, w_in, b_in, q_gain, k_gain):
    bsz, seq, d = x.shape
    tm = INPROJ_TILE
    n_rest = w_in.shape[1] - HY_IN
    n_gate = n_rest - ATT_WIDTH - 2 * KV_WIDTH
    whyT = w_in[:, :HY_IN].T.astype(BF16)
    wr = w_in[:, HY_IN:].astype(BF16)
    bhy = b_in[:HY_IN].reshape(HY_IN, 1)
    br = b_in[HY_IN:].reshape(1, n_rest)
    cos, sin = _rope_tables(seq)
    reps = LANES // HEAD_DIM
    qg = jnp.tile(q_gain, reps).reshape(1, LANES)
    kg = jnp.tile(k_gain, reps).reshape(1, LANES)
    bd = jnp.asarray(np.kron(np.eye(reps, dtype=np.float32), np.ones((HEAD_DIM, HEAD_DIM), np.float32))).astype(BF16)
    full = lambda shape: pl.BlockSpec(shape, lambda b, i: (0,) * len(shape))
    return pl.pallas_call(
        _inproj_kernel,
        grid=(bsz, seq // tm),
        in_specs=[pl.BlockSpec((1, tm, d), lambda b, i: (b, i, 0)),
                  pl.BlockSpec((1, 6, d), lambda b, i: (b, 0, 0)),
                  full((1, d)), full((HY_IN, d)), full((HY_IN, 1)), full((d, n_rest)), full((1, n_rest)),
                  pl.BlockSpec((tm, LANES), lambda b, i: (i, 0)),
                  pl.BlockSpec((tm, LANES), lambda b, i: (i, 0)),
                  full((1, LANES)), full((1, LANES)), full((LANES, LANES))],
        out_specs=[pl.BlockSpec((1, HY_IN, tm), lambda b, i: (b, 0, i)),
                   pl.BlockSpec((1, N_HEADS, tm, HEAD_DIM), lambda b, i: (b, 0, i, 0)),
                   pl.BlockSpec((1, N_KV_HEADS, tm, HEAD_DIM), lambda b, i: (b, 0, i, 0)),
                   pl.BlockSpec((1, N_KV_HEADS, HEAD_DIM + ONES_ROWS, tm), lambda b, i: (b, 0, 0, i)),
                   pl.BlockSpec((1, tm, n_gate), lambda b, i: (b, i, 0))],
        out_shape=[jax.ShapeDtypeStruct((bsz, HY_IN, seq), F32),
                   jax.ShapeDtypeStruct((bsz, N_HEADS, seq, HEAD_DIM), BF16),
                   jax.ShapeDtypeStruct((bsz, N_KV_HEADS, seq, HEAD_DIM), BF16),
                   jax.ShapeDtypeStruct((bsz, N_KV_HEADS, HEAD_DIM + ONES_ROWS, seq), BF16),
                   jax.ShapeDtypeStruct((bsz, seq, n_gate), F32)],
        compiler_params=_params("parallel", "parallel"),
        name="inproj",
    )(x, mod3, g_mix.reshape(1, d), whyT, bhy, wr, br, cos, sin, qg, kg, bd)


def _filter_kernel(zT_ref, w1T_ref, b1_ref, w2T_ref, b2_ref, w3T_ref, b3_ref, fr_ref, w4T_ref,
                   t01_ref, absd_ref, o_ref, hdn_ref):
    @pl.when(pl.program_id(0) == 0)
    def _():
        fr = fr_ref[...]
        h = jnp.sin(fr[:, 0:1] * (jnp.dot(w1T_ref[...], zT_ref[...], precision=HIGHEST,
                                          preferred_element_type=F32) + b1_ref[...]))
        h = jnp.sin(fr[:, 1:2] * (jnp.dot(w2T_ref[...], h, precision=HIGHEST,
                                          preferred_element_type=F32) + b2_ref[...]))
        h = jnp.sin(fr[:, 2:3] * (jnp.dot(w3T_ref[...], h, precision=HIGHEST,
                                          preferred_element_type=F32) + b3_ref[...]))
        hdn_ref[...] = h

    w_hi, w_lo = _split2(w4T_ref[...])
    h_hi, h_lo = _split2(hdn_ref[...])
    hT = (jnp.dot(w_hi, h_hi, preferred_element_type=F32) + jnp.dot(w_lo, h_hi, preferred_element_type=F32)
          + jnp.dot(w_hi, h_lo, preferred_element_type=F32))
    hT = hT * jnp.exp(-t01_ref[...] * absd_ref[...])
    nrm = jnp.sum(jnp.abs(hT), axis=1, keepdims=True) + EPS
    o_ref[...] = hT * (1.0 / nrm)


def _filters(seq, w1, b1, w2, b2, w3, b3, w4, freq):
    emb_pad = 64
    bands = (FILTER_EMB - 1) // 2
    t01 = np.linspace(0.0, 1.0, seq, dtype=np.float32)[None, :]
    f = np.linspace(1e-4, bands - 1, bands, dtype=np.float32)[:, None]
    w = ((2.0 * math.pi) * np.arange(seq, dtype=np.float32) / seq).astype(np.float32)[None, :]
    zT = np.zeros((emb_pad, seq), np.float32)
    zT[0:1] = t01
    zT[1:1 + bands] = np.cos(f * w)
    zT[1 + bands:1 + 2 * bands] = -np.sin(f * w)
    w1T = jnp.zeros((FILTER_HIDDEN, emb_pad), F32).at[:, :FILTER_EMB].set(w1.T)
    max_decay = math.log(HY_DECAY_TARGET) / HY_QUICK_DECAY_PCT
    min_decay = math.log(HY_DECAY_TARGET) / HY_GRADUAL_DECAY_PCT
    deltas = np.abs(np.linspace(min_decay, max_decay, HY_WIDTH, dtype=np.float32))
    n_rows = HY_ORDER * 2 * HY_WIDTH
    absd = np.tile(deltas, HY_ORDER * 2).reshape(n_rows, 1)
    rb = FILTER_ROWS
    full = lambda shape: pl.BlockSpec(shape, lambda j: (0,) * len(shape))
    col = lambda v: v.reshape(FILTER_HIDDEN, 1)
    return pl.pallas_call(
        _filter_kernel,
        grid=(n_rows // rb,),
        in_specs=[full((emb_pad, seq)),
                  full((FILTER_HIDDEN, emb_pad)), full((FILTER_HIDDEN, 1)),
                  full((FILTER_HIDDEN, FILTER_HIDDEN)), full((FILTER_HIDDEN, 1)),
                  full((FILTER_HIDDEN, FILTER_HIDDEN)), full((FILTER_HIDDEN, 1)),
                  full((FILTER_HIDDEN, 3)),
                  pl.BlockSpec((rb, FILTER_HIDDEN), lambda j: (j, 0)),
                  full((1, seq)),
                  pl.BlockSpec((rb, 1), lambda j: (j, 0))],
        out_specs=pl.BlockSpec((rb, seq), lambda j: (j, 0)),
        out_shape=jax.ShapeDtypeStruct((n_rows, seq), F32),
        scratch_shapes=[pltpu.VMEM((FILTER_HIDDEN, seq), F32)],
        compiler_params=_params("arbitrary"),
        name="filters",
    )(jnp.asarray(zT), w1T, col(b1), w2.T, col(b2), w3.T, col(b3), freq.T, w4.T,
      jnp.asarray(t01), jnp.asarray(absd))


def _dft_tables(seq):
    n = 2 * seq
    r = int(round(math.sqrt(n)))
    assert r * r == n, "2*seq must be a perfect square"
    hh = r // 2
    rh = -(-(hh + 1) // BF16_SUBLANES) * BF16_SUBLANES
    k = np.arange(r, dtype=np.float64)
    ang = 2.0 * np.pi * np.outer(k, k) / r
    fr, fi = np.cos(ang), -np.sin(ang)
    angt = 2.0 * np.pi * np.outer(k, k) / n
    keep = np.zeros((rh, 1))
    keep[:hh + 1] = 1.0
    pad = lambda a: np.concatenate([a[:hh + 1], np.zeros((rh - hh - 1,) + a.shape[1:])], axis=0)
    twr, twi = pad(np.cos(angt)), pad(-np.sin(angt))
    f1 = np.concatenate([pad(fr[:, :hh]), pad(fi[:, :hh])], axis=0)
    f3 = np.block([[fr, fi], [-fi, fr]])
    f3i = np.block([[fr, -fi], [fi, fr]])
    weight = np.full((rh, 1), 2.0) * keep
    weight[0] = weight[hh] = 1.0
    f1i = np.concatenate([(pad(fr[:, :hh]) * weight).T, (pad(fi[:, :hh]) * weight).T], axis=1) / n
    as32 = lambda a: jnp.asarray(a.astype(np.float32))
    return as32(f1), as32(twr), as32(twi), as32(f3), as32(f3i), as32(f1i)


def _hyena_kernel(sw_ref, sb_ref, fb_ref, v_ref, x1_ref, x2_ref, h_ref, f1_f32, twr_ref, twi_ref,
                  f3_f32, f3i_f32, f1i_f32, o_ref, f1_ref, f3_ref, f3i_ref, f1i_ref):
    bsz, cb, hh, r = v_ref.shape
    rh = twr_ref.shape[0]
    j = pl.program_id(0)
    f1_ref[...] = f1_f32[...].astype(BF16)
    f3_ref[...] = f3_f32[...].astype(BF16)
    f3i_ref[...] = f3i_f32[...].astype(BF16)
    f1i_ref[...] = f1i_f32[...].astype(BF16)
    twr = twr_ref[...]
    twi = twi_ref[...]
    lane = lax.broadcasted_iota(I32, (hh, r), 1)
    row = lax.broadcasted_iota(I32, (hh, r), 0)

    def shift_prev(z):
        a = pltpu.roll(z, 1, 1)
        a = jnp.where(lane == 0, pltpu.roll(a, 1, 0), a)
        return jnp.where((lane == 0) & (row == 0), 0.0, a)

    def shift_next(z):
        a = pltpu.roll(z, r - 1, 1)
        a = jnp.where(lane == r - 1, pltpu.roll(a, hh - 1, 0), a)
        return jnp.where((lane == r - 1) & (row == hh - 1), 0.0, a)

    def fwd(slabs):
        m = len(slabs)
        acat = jnp.concatenate([a.astype(BF16) for a in slabs], axis=1)
        s = jnp.dot(f1_ref[...], acat, preferred_element_type=F32)
        lhs = []
        for i in range(m):
            br = s[:rh, i * r:(i + 1) * r]
            bi = s[rh:, i * r:(i + 1) * r]
            cr = br * twr - bi * twi
            ci = br * twi + bi * twr
            lhs.append(jnp.concatenate([cr, ci], axis=1).astype(BF16))
        d = jnp.dot(jnp.concatenate(lhs, axis=0), f3_ref[...], preferred_element_type=F32)
        return [(d[i * rh:(i + 1) * rh, :r], d[i * rh:(i + 1) * rh, r:]) for i in range(m)]

    def inv(specs):
        m = len(specs)
        lhs = jnp.concatenate([jnp.concatenate([pr, pi], axis=1).astype(BF16) for pr, pi in specs], axis=0)
        e = jnp.dot(lhs, f3i_ref[...], preferred_element_type=F32)
        rhs = []
        for i in range(m):
            er = e[i * rh:(i + 1) * rh, :r]
            ei = e[i * rh:(i + 1) * rh, r:]
            tr = er * twr + ei * twi
            ti = ei * twr - er * twi
            rhs.append(jnp.concatenate([tr, ti], axis=0).astype(BF16))
        y = jnp.dot(f1i_ref[...], jnp.concatenate(rhs, axis=1), preferred_element_type=F32)
        return [y[:, i * r:(i + 1) * r] for i in range(m)]

    def cmul(a, g):
        return a[0] * g[0] - a[1] * g[1], a[0] * g[1] + a[1] * g[0]

    def sconv(ref, b, c, gi):
        z = ref[b, c]
        return (sb_ref[gi] + sw_ref[gi] * shift_prev(z) + sw_ref[HY_IN + gi] * z
                + sw_ref[2 * HY_IN + gi] * shift_next(z))

    nb = HY_CH_BATCH

    def body(t, carry):
        cs = [t * nb + i for i in range(nb)]
        chs = [j * cb + c for c in cs]
        pairs = [(c, ch, b) for c, ch in zip(cs, chs) for b in range(bsz)]
        v = [sconv(v_ref, b, c, ch) for c, ch, b in pairs]
        filt = [h_ref[o, d, c] for c in cs for o in range(HY_ORDER) for d in range(2)]
        spec = fwd(v + filt)
        nv = len(v)
        g = []
        for i in range(nb):
            gi = []
            for o in range(HY_ORDER):
                hf = spec[nv + (i * HY_ORDER + o) * 2]
                hb = spec[nv + (i * HY_ORDER + o) * 2 + 1]
                gi.append((hf[0] + hb[0], hf[1] - hb[1]))
            g.append(gi)
        conv = inv([cmul(spec[n], g[n // bsz][0]) for n in range(nv)])
        y = [sconv(x1_ref, b, c, HY_WIDTH + ch) * (conv[n] + fb_ref[ch] * v[n])
             for n, (c, ch, b) in enumerate(pairs)]
        spec = fwd(y)
        conv = inv([cmul(spec[n], g[n // bsz][1]) for n in range(nv)])
        for n, (c, ch, b) in enumerate(pairs):
            o_ref[b, c] = (sconv(x2_ref, b, c, 2 * HY_WIDTH + ch)
                           * (conv[n] + fb_ref[HY_WIDTH + ch] * y[n]))
        return carry

    lax.fori_loop(0, cb // nb, body, 0)


def _hyena(zhyT, hT, short_w, short_b, filt_bias):
    bsz, _, seq = zhyT.shape
    tabs = _dft_tables(seq)
    r = tabs[1].shape[1]
    hh = r // 2
    z4 = zhyT.reshape(bsz, HY_IN, hh, r)
    h5 = hT.reshape(HY_ORDER, 2, HY_WIDTH, hh, r)
    cb = HY_CH_BLOCK
    nblk = HY_WIDTH // cb
    smem = pl.BlockSpec(memory_space=pltpu.SMEM)
    full = lambda a: pl.BlockSpec(a.shape, lambda j: (0,) * a.ndim)
    zspec = lambda off: pl.BlockSpec((bsz, cb, hh, r), lambda j: (0, off * nblk + j, 0, 0))
    out = pl.pallas_call(
        _hyena_kernel,
        grid=(nblk,),
        in_specs=[smem, smem, smem, zspec(0), zspec(1), zspec(2),
                  pl.BlockSpec((HY_ORDER, 2, cb, hh, r), lambda j: (0, 0, j, 0, 0))]
                 + [full(t) for t in tabs],
        out_specs=pl.BlockSpec((bsz, cb, hh, r), lambda j: (0, j, 0, 0)),
        out_shape=jax.ShapeDtypeStruct((bsz, HY_WIDTH, hh, r), F32),
        scratch_shapes=[pltpu.VMEM(tabs[i].shape, BF16) for i in (0, 3, 4, 5)],
        compiler_params=_params("parallel"),
        name="hyena",
    )(short_w.reshape(-1), short_b, filt_bias.reshape(-1), z4, z4, z4, h5, *tabs)
    return out.reshape(bsz, HY_WIDTH, seq)


def _attn_kernel(bounded_ref, q_ref, k_ref, vt_ref, o_ref):
    g, tq, hd = q_ref.shape[1:]
    seq = k_ref.shape[2]
    rows = vt_ref.shape[2]
    n_chunks = seq // KV_CHUNK
    q = q_ref[0].reshape(g * tq, hd)

    def scores(kc):
        kb = k_ref[0, 0, kc * KV_CHUNK:(kc + 1) * KV_CHUNK, :]
        return lax.dot_general(kb, q, NT_DIMS, preferred_element_type=F32)

    def values(kc):
        return vt_ref[0, 0, :, kc * KV_CHUNK:(kc + 1) * KV_CHUNK]

    def finish(acc):
        out = acc[:hd] / acc[hd:hd + 1]
        o_ref[0] = out.T.reshape(g, tq, hd).astype(o_ref.dtype)

    @pl.when(bounded_ref[0] == 1)
    def _():
        acc = jnp.zeros((rows, g * tq), F32)
        for kc in range(n_chunks):
            p = jnp.exp2(scores(kc)).astype(BF16)
            acc = acc + jnp.dot(values(kc), p, preferred_element_type=F32)
        finish(acc)

    @pl.when(bounded_ref[0] != 1)
    def _():
        m = jnp.full((1, g * tq), -jnp.inf, F32)
        acc = jnp.zeros((rows, g * tq), F32)
        s_next = scores(0)
        for kc in range(n_chunks):
            s = s_next
            if kc + 1 < n_chunks:
                s_next = scores(kc + 1)
            m_new = jnp.maximum(m, jnp.max(s, axis=0, keepdims=True))
            p = jnp.exp2(s - m_new).astype(BF16)
            acc = jnp.exp2(m - m_new) * acc + jnp.dot(values(kc), p, preferred_element_type=F32)
            m = m_new
        finish(acc)


def _attention(qh, kh, vt, q_gain, k_gain):
    bsz, _, seq, _ = qh.shape
    g = N_HEADS // N_KV_HEADS
    rows = vt.shape[2]
    bound = HEAD_DIM * Q_SCALE * BF16_NORM_SLACK * jnp.max(jnp.abs(q_gain)) * jnp.max(jnp.abs(k_gain))
    bounded = (bound <= SAFE_SCORE_BOUND).astype(I32).reshape(1)
    tq = Q_TILE
    grid_spec = pltpu.PrefetchScalarGridSpec(
        num_scalar_prefetch=1,
        grid=(bsz, N_KV_HEADS, seq // tq),
        in_specs=[pl.BlockSpec((1, g, tq, HEAD_DIM), lambda b, h, i, f: (b, h, i, 0)),
                  pl.BlockSpec((1, 1, seq, HEAD_DIM), lambda b, h, i, f: (b, h, 0, 0)),
                  pl.BlockSpec((1, 1, rows, seq), lambda b, h, i, f: (b, h, 0, 0))],
        out_specs=pl.BlockSpec((1, g, tq, HEAD_DIM), lambda b, h, i, f: (b, h, i, 0)))
    return pl.pallas_call(
        _attn_kernel,
        grid_spec=grid_spec,
        out_shape=jax.ShapeDtypeStruct((bsz, N_HEADS, seq, HEAD_DIM), BF16),
        compiler_params=_params("parallel", "parallel", "parallel"),
        name="attention",
    )(bounded, qh, kh, vt)


def _merge_kernel(x_ref, yhyT_ref, yat_ref, gate_ref, mod_ref, g_ref, why_ref, wat_ref, wout_ref, wr_ref,
                  x1_ref, h2_ref, aff_ref):
    d = x_ref.shape[2]
    yhy = yhyT_ref[0].T.astype(BF16)
    a = jnp.dot(yhy, why_ref[...], preferred_element_type=F32)
    yat = jnp.concatenate([yat_ref[0, h] for h in range(yat_ref.shape[1])], axis=1)
    b = jnp.dot(yat, wat_ref[...], preferred_element_type=F32)
    gate = gate_ref[0]
    mrg = gate[:, :d] * a + gate[:, d:] * b
    o = jnp.dot(mrg.astype(BF16), wout_ref[...], preferred_element_type=F32)
    gt1 = mod_ref[0, 2:3, :]
    sh2 = mod_ref[0, 3:4, :]
    sc2 = mod_ref[0, 4:5, :]
    x1 = x_ref[0] + gt1 * o
    x1_ref[0] = x1
    ms = jnp.mean(x1 * x1, axis=-1, keepdims=True)
    h2 = (x1 * lax.rsqrt(ms + EPS)) * g_ref[...] * (1.0 + sc2) + sh2
    hi, lo = _split2(h2)
    h2_ref[0] = hi
    w_hi, w_lo = _split2(wr_ref[...])
    logits = (jnp.dot(hi, w_hi, preferred_element_type=F32)
              + jnp.dot(lo, w_hi, preferred_element_type=F32)
              + jnp.dot(hi, w_lo, preferred_element_type=F32))
    ex = jnp.exp(logits - jnp.max(logits, axis=-1, keepdims=True))
    aff_ref[0] = ex / jnp.sum(ex, axis=-1, keepdims=True)


def _merge(x, yhyT, yat, gates, mod3, g_ffn, w_hy_out, w_att_out, w_out, w_router):
    bsz, seq, d = x.shape
    tm = TOKEN_TILE
    ne = w_router.shape[1]
    full = lambda shape: pl.BlockSpec(shape, lambda b, i: (0,) * len(shape))
    return pl.pallas_call(
        _merge_kernel,
        grid=(bsz, seq // tm),
        in_specs=[pl.BlockSpec((1, tm, d), lambda b, i: (b, i, 0)),
                  pl.BlockSpec((1, HY_WIDTH, tm), lambda b, i: (b, 0, i)),
                  pl.BlockSpec((1, N_HEADS, tm, HEAD_DIM), lambda b, i: (b, 0, i, 0)),
                  pl.BlockSpec((1, tm, 2 * d), lambda b, i: (b, i, 0)),
                  pl.BlockSpec((1, 6, d), lambda b, i: (b, 0, 0)),
                  full((1, d)), full((HY_WIDTH, d)), full((ATT_WIDTH, d)), full((d, d)), full((d, ne))],
        out_specs=[pl.BlockSpec((1, tm, d), lambda b, i: (b, i, 0)),
                   pl.BlockSpec((1, tm, d), lambda b, i: (b, i, 0)),
                   pl.BlockSpec((1, tm, ne), lambda b, i: (b, i, 0))],
        out_shape=[jax.ShapeDtypeStruct((bsz, seq, d), F32),
                   jax.ShapeDtypeStruct((bsz, seq, d), BF16),
                   jax.ShapeDtypeStruct((bsz, seq, ne), F32)],
        compiler_params=_params("parallel", "parallel"),
        name="merge",
    )(x, yhyT, yat, gates, mod3, g_ffn.reshape(1, d), w_hy_out.astype(BF16), w_att_out.astype(BF16),
      w_out.astype(BF16), w_router)


def _topk_kernel(a_ref, ut_ref, pos_ref, *, cap):
    nr, seq = a_ref.shape
    a = a_ref[...]

    def count(mask):
        return jnp.sum(mask.astype(F32), axis=1, keepdims=True)

    def probe(t, lo, hi):
        ok = count(a >= t) >= cap
        return ok, jnp.where(ok, t, lo), jnp.where(ok, hi, t)

    def by_exponent(_, c):
        elo, ehi, lo, hi = c
        emid = 0.5 * (elo + ehi)
        ok, lo, hi = probe(jnp.exp2(emid), lo, hi)
        return jnp.where(ok, emid, elo), jnp.where(ok, ehi, emid), lo, hi

    def by_value(_, c):
        lo, hi = c
        _, lo, hi = probe(0.5 * (lo + hi), lo, hi)
        return lo, hi

    col = lambda v: jnp.full((nr, 1), v, F32)
    _, _, lo, hi = lax.fori_loop(0, TOPK_EXP_STEPS, by_exponent, (col(F32_MIN_EXP), col(1.0), col(0.0), col(2.0)))
    lo, hi = lax.fori_loop(0, TOPK_VALUE_STEPS, by_value, (lo, hi))
    gt = a >= hi
    eq = (a >= lo) & (a < hi)
    need = cap - count(gt)
    ut = ut_ref[...]
    run_eq = jnp.zeros((nr, 1), F32)
    run_sel = jnp.zeros((nr, 1), F32)
    for ch in range(seq // LANES):
        sl = slice(ch * LANES, (ch + 1) * LANES)
        eq_rank = jnp.dot(eq[:, sl].astype(BF16), ut, preferred_element_type=F32) + run_eq
        run_eq = eq_rank[:, LANES - 1:LANES]
        sel = gt[:, sl] | (eq[:, sl] & (eq_rank <= need))
        pos = jnp.dot(sel.astype(BF16), ut, preferred_element_type=F32) + run_sel
        run_sel = pos[:, LANES - 1:LANES]
        pos_ref[:, sl] = jnp.where(sel, pos.astype(I32) - 1, -1)


def _topk(aff2, cap):
    nr, seq = aff2.shape
    ut = jnp.asarray(np.triu(np.ones((LANES, LANES), np.float32))).astype(BF16)
    return pl.pallas_call(
        functools.partial(_topk_kernel, cap=cap),
        grid=(1,),
        in_specs=[pl.BlockSpec((nr, seq), lambda i: (0, 0)), pl.BlockSpec((LANES, LANES), lambda i: (0, 0))],
        out_specs=pl.BlockSpec((nr, seq), lambda i: (0, 0)),
        out_shape=jax.ShapeDtypeStruct((nr, seq), I32),
        compiler_params=_params("arbitrary"),
        name="topk",
    )(aff2, ut)


def _window_start(st_ref, idx, w, cap):
    base = st_ref[idx] + w * WINDOW
    return base, pl.multiple_of(jnp.minimum(base, cap - WINDOW), BF16_SUBLANES)


def _gather_kernel(st_ref, np_ref, pos_ref, h2_ref, xe_ref):
    eh = pos_ref.shape[1]
    tile = TOKEN_TILE
    per_step = pos_ref.shape[2] // tile
    cap = xe_ref.shape[2]
    steps = pl.num_programs(2)
    nt = steps * per_step
    b, half, i = pl.program_id(0), pl.program_id(1), pl.program_id(2)
    ne = pl.num_programs(1) * eh

    @pl.when(i == 0)
    def _():
        xe_ref[...] = jnp.zeros_like(xe_ref)

    rho = lax.broadcasted_iota(I32, (WINDOW, tile), 0)

    def one_pass(w, carry):
        starts, got = [], []
        for t in range(per_step):
            tok = slice(t * tile, (t + 1) * tile)
            rows = []
            for el in range(eh):
                base, st = _window_start(st_ref, (b * ne + half * eh + el) * nt + i * per_step + t, w, cap)
                pos = pos_ref[0, el:el + 1, tok]
                rows.append(((pos - st == rho) & (pos >= base)).astype(BF16))
                starts.append(st)
            got.append(jnp.dot(jnp.concatenate(rows, axis=0), h2_ref[0, tok, :],
                               preferred_element_type=F32).astype(BF16))
        for t in range(per_step):
            for el in range(eh):
                xe_ref[0, el, pl.ds(starts[t * eh + el], WINDOW), :] += got[t][el * WINDOW:(el + 1) * WINDOW]
        return carry

    lax.fori_loop(0, np_ref[(b * pl.num_programs(1) + half) * steps + i], one_pass, 0)


def _gather(st16, npass_gather, pos, h2, cap):
    bsz, ne, seq = pos.shape
    d = h2.shape[2]
    tile = GATHER_TILES * TOKEN_TILE
    eh = ne // GATHER_SPLIT
    grid_spec = pltpu.PrefetchScalarGridSpec(
        num_scalar_prefetch=2,
        grid=(bsz, GATHER_SPLIT, seq // tile),
        in_specs=[pl.BlockSpec((1, eh, tile), lambda b, h, i, st, n: (b, h, i)),
                  pl.BlockSpec((1, tile, d), lambda b, h, i, st, n: (b, i, 0))],
        out_specs=pl.BlockSpec((1, eh, cap, d), lambda b, h, i, st, n: (b, h, 0, 0)))
    return pl.pallas_call(
        _gather_kernel,
        grid_spec=grid_spec,
        out_shape=jax.ShapeDtypeStruct((bsz, ne, cap, d), BF16),
        compiler_params=_params("parallel", "parallel", "arbitrary"),
        name="gather",
    )(st16, npass_gather, pos, h2)


def _moe_kernel(xe_ref, wg_ref, wu_ref, wd_ref, ye_ref, wg_s, wu_s, wd_s):
    tile = TOKEN_TILE
    k = pl.program_id(0)
    b = pl.program_id(1)
    ne = pl.num_programs(0) - 1
    rows_in = wg_ref.shape[1]
    rows_ff = wd_ref.shape[1]

    @pl.when(k < ne)
    def _():
        nxt = k % 2
        wg_s[nxt, pl.ds(pl.multiple_of(b * rows_in, rows_in), rows_in), :] = wg_ref[0].astype(BF16)
        wu_s[nxt, pl.ds(pl.multiple_of(b * rows_in, rows_in), rows_in), :] = wu_ref[0].astype(BF16)
        wd_s[nxt, pl.ds(pl.multiple_of(b * rows_ff, rows_ff), rows_ff), :] = wd_ref[0].astype(BF16)

    @pl.when(k == 0)
    def _():
        ye_ref[...] = jnp.zeros_like(ye_ref)

    @pl.when(k > 0)
    def _():
        cur = (k + 1) % 2
        for r in range(xe_ref.shape[2] // tile):
            xe = xe_ref[0, 0, r * tile:(r + 1) * tile, :]
            a = jnp.dot(xe, wg_s[cur], preferred_element_type=F32)
            u = jnp.dot(xe, wu_s[cur], preferred_element_type=F32)
            hmid = (a * jax.nn.sigmoid(a) * u).astype(BF16)
            ye_ref[0, 0, r * tile:(r + 1) * tile, :] = jnp.dot(hmid, wd_s[cur],
                                                              preferred_element_type=F32).astype(BF16)


def _moe(xe, wg, wu, wd):
    bsz, ne, cap, d = xe.shape
    dff = wg.shape[2]
    assert d % bsz == 0 and dff % bsz == 0
    prev = lambda k: jnp.maximum(k - 1, 0)
    this = lambda k: jnp.minimum(k, ne - 1)
    return pl.pallas_call(
        _moe_kernel,
        grid=(ne + 1, bsz),
        in_specs=[pl.BlockSpec((1, 1, cap, d), lambda k, b: (b, prev(k), 0, 0)),
                  pl.BlockSpec((1, d // bsz, dff), lambda k, b: (this(k), b, 0)),
                  pl.BlockSpec((1, d // bsz, dff), lambda k, b: (this(k), b, 0)),
                  pl.BlockSpec((1, dff // bsz, d), lambda k, b: (this(k), b, 0))],
        out_specs=pl.BlockSpec((1, 1, cap, d), lambda k, b: (b, jnp.where(k == 0, ne, k - 1), 0, 0)),
        out_shape=jax.ShapeDtypeStruct((bsz, ne + 1, cap, d), BF16),
        scratch_shapes=[pltpu.VMEM((2, d, dff), BF16), pltpu.VMEM((2, d, dff), BF16),
                        pltpu.VMEM((2, dff, d), BF16)],
        compiler_params=_params("arbitrary", "arbitrary"),
        name="moe",
    )(xe, wg, wu, wd)


def _combine_kernel(st_ref, np_ref, pos_ref, aff_ref, ye_ref, x1_ref, mod_ref, ex_ref, rho_ref, o_ref,
                    rhs_ref, acc_ref):
    ne = pos_ref.shape[2]
    tile = TOKEN_TILE
    per_step = pos_ref.shape[1] // tile
    cap = ye_ref.shape[2]
    steps = pl.num_programs(1)
    nt = steps * per_step
    b, i = pl.program_id(0), pl.program_id(1)
    acc_ref[...] = jnp.zeros_like(acc_ref)
    lane = lax.broadcasted_iota(I32, (1, ne), 1)
    aexp = jnp.dot(aff_ref[0].astype(BF16), ex_ref[...], preferred_element_type=F32)

    def one_pass(w, carry):
        for t in range(per_step):
            tok = slice(t * tile, (t + 1) * tile)
            base_v = jnp.zeros((1, ne), I32)
            st_v = jnp.zeros((1, ne), I32)
            for e in range(ne):
                base, st = _window_start(st_ref, (b * ne + e) * nt + i * per_step + t, w, cap)
                rhs_ref[t, e * WINDOW:(e + 1) * WINDOW, :] = ye_ref[0, e, pl.ds(st, WINDOW), :]
                base_v = jnp.where(lane == e, base, base_v)
                st_v = jnp.where(lane == e, st, st_v)
            pos = pos_ref[0, tok, :]
            off = pos - st_v
            off = jnp.where((pos >= base_v) & (off >= 0) & (off < WINDOW), off, -1)
            oexp = jnp.dot(off.astype(F32).astype(BF16), ex_ref[...], preferred_element_type=F32)
            pt = jnp.where(oexp == rho_ref[...], aexp[tok], 0.0).astype(BF16)
            acc_ref[tok, :] += jnp.dot(pt, rhs_ref[t], preferred_element_type=F32)
        return carry

    lax.fori_loop(0, np_ref[b * steps + i], one_pass, 0)
    o_ref[0] = x1_ref[0] + mod_ref[0, 5:6, :] * acc_ref[...]


def _combine(st16, npass, pos_tok, aff_tok, ye, x1, mod3):
    bsz, seq, d = x1.shape
    ne = aff_tok.shape[2]
    cap = ye.shape[2]
    tile = COMBINE_TILES * TOKEN_TILE
    spread = np.kron(np.eye(ne, dtype=np.float32), np.ones((1, WINDOW), np.float32))
    rho = np.tile(np.arange(WINDOW, dtype=np.float32), ne).reshape(1, ne * WINDOW)
    grid_spec = pltpu.PrefetchScalarGridSpec(
        num_scalar_prefetch=2,
        grid=(bsz, seq // tile),
        in_specs=[pl.BlockSpec((1, tile, ne), lambda b, i, st, n: (b, i, 0)),
                  pl.BlockSpec((1, tile, ne), lambda b, i, st, n: (b, i, 0)),
                  pl.BlockSpec((1, ne, cap, d), lambda b, i, st, n: (b, 0, 0, 0), pipeline_mode=pl.Buffered(1)),
                  pl.BlockSpec((1, tile, d), lambda b, i, st, n: (b, i, 0)),
                  pl.BlockSpec((1, 6, d), lambda b, i, st, n: (b, 0, 0)),
                  pl.BlockSpec((ne, ne * WINDOW), lambda b, i, st, n: (0, 0)),
                  pl.BlockSpec((1, ne * WINDOW), lambda b, i, st, n: (0, 0))],
        out_specs=pl.BlockSpec((1, tile, d), lambda b, i, st, n: (b, i, 0)),
        scratch_shapes=[pltpu.VMEM((COMBINE_TILES, ne * WINDOW, d), BF16), pltpu.VMEM((tile, d), F32)])
    return pl.pallas_call(
        _combine_kernel,
        grid_spec=grid_spec,
        out_shape=jax.ShapeDtypeStruct((bsz, seq, d), F32),
        compiler_params=_params("parallel", "arbitrary"),
        name="combine",
    )(st16, npass, pos_tok, aff_tok, ye, x1, mod3, jnp.asarray(spread).astype(BF16), jnp.asarray(rho))


def _route_tables(pos):
    bsz, ne, seq = pos.shape
    tile = TOKEN_TILE
    counts = jnp.sum((pos >= 0).reshape(bsz, ne, seq // tile, tile), axis=-1, dtype=I32)
    starts = jnp.cumsum(counts, axis=-1) - counts
    st16 = (starts // BF16_SUBLANES) * BF16_SUBLANES
    span = jnp.where(counts > 0, starts - st16 + counts, 0)
    passes = (span + WINDOW - 1) // WINDOW
    npass = jnp.max(passes.reshape(bsz, ne, -1, COMBINE_TILES), axis=(1, 3))
    npass_gather = jnp.max(passes.reshape(bsz, GATHER_SPLIT, ne // GATHER_SPLIT, -1, GATHER_TILES), axis=(2, 4))
    return st16.reshape(-1), npass.reshape(-1), npass_gather.reshape(-1)


def kernel(x, c, w_ada, b_ada, g_mix, g_ffn, w_in, b_in, short_w, short_b, hy_w1, hy_b1, hy_w2, hy_b2, hy_w3, hy_b3, hy_w4, hy_freq, hy_bias, q_gain, k_gain, w_hy_out, w_att_out, w_out, w_router, w_gate, w_up, w_down):
    bsz, seq, d = x.shape
    depth = w_ada.shape[0]
    ne = w_router.shape[-1]
    cap = EC_FACTOR * seq // ne
    for l in range(depth):
        mod3 = _adaln(c, w_ada[l], b_ada[l]).reshape(bsz, 6, d)
        zhyT, q, k, v, gates = _inproj(x, mod3, g_mix[l], w_in[l], b_in[l], q_gain[l], k_gain[l])
        hT = _filters(seq, hy_w1[l], hy_b1[l], hy_w2[l], hy_b2[l], hy_w3[l], hy_b3[l], hy_w4[l], hy_freq[l])
        yhyT = _hyena(zhyT, hT, short_w[l], short_b[l], hy_bias[l])
        yat = _attention(q, k, v, q_gain[l], k_gain[l])
        x1, h2, aff = _merge(x, yhyT, yat, gates, mod3, g_ffn[l], w_hy_out[l], w_att_out[l], w_out[l],
                             w_router[l])
        pos = _topk(aff.transpose(0, 2, 1).reshape(bsz * ne, seq), cap).reshape(bsz, ne, seq)
        st16, npass, npass_gather = _route_tables(pos)
        xe = _gather(st16, npass_gather, pos, h2, cap)
        ye = _moe(xe, w_gate[l], w_up[l], w_down[l])
        x = _combine(st16, npass, pos.transpose(0, 2, 1), aff, ye, x1, mod3)
    return x
```
